```python
import math, functools
import jax, jax.numpy as jnp
from jax import lax
import numpy as np

D_MODEL = 1024
BATCH = 4
SEQ = 8192
DEPTH = 1

HEAD_DIM = 64
N_HEADS_TOTAL = D_MODEL // HEAD_DIM
N_ATT_HEADS = N_HEADS_TOTAL // 2
N_RWKV_HEADS = N_HEADS_TOTAL - N_ATT_HEADS
ATT_WIDTH = N_ATT_HEADS * HEAD_DIM
RWKV_WIDTH = N_RWKV_HEADS * HEAD_DIM
MIX_WIDTH = ATT_WIDTH + RWKV_WIDTH
ROPE_DIM = HEAD_DIM // 4
ROPE_THETA = 500000.0
DILATED_PATTERNS = ((128, 1), (512, 4), (2048, 16))
SEQ_ALIGN = functools.reduce(math.lcm, [w for w, _ in DILATED_PATTERNS])
DECAY_LORA = 64
AAA_LORA = 64
GATE_LORA = 128
ATT_COLS = 3 * ATT_WIDTH
SHIFT_COLS = 3 * RWKV_WIDTH + DECAY_LORA + AAA_LORA + GATE_LORA
IN_COLS = ATT_COLS + SHIFT_COLS
FFN_HIDDEN = -(-8 * D_MODEL // (3 * 256)) * 256
ALPHA = (2.0 * DEPTH) ** 0.25
BETA = (8.0 * DEPTH) ** -0.25
LN_EPS = 1e-5
GN_EPS = 64e-5
L2_EPS = 1e-6
DECAY_SCALE = math.exp(-0.5)

kernel_name = 'hymba_dilated_attn_rwkv7_deepnorm'


def layer_norm(x, g, b):
    xf = x.astype(jnp.float32)
    mu = jnp.mean(xf, axis=-1, keepdims=True)
    var = jnp.mean(jnp.square(xf - mu), axis=-1, keepdims=True)
    return ((xf - mu) * lax.rsqrt(var + LN_EPS) * g + b).astype(x.dtype)


def partial_rope(t, positions):
    half = ROPE_DIM // 2
    inv_freq = ROPE_THETA ** (-jnp.arange(half, dtype=jnp.float32) / half)
    ang = positions.astype(jnp.float32)[:, None, :, None] * inv_freq
    cos, sin = jnp.cos(ang), jnp.sin(ang)
    tf = t.astype(jnp.float32)
    x1, x2 = tf[..., :half], tf[..., half:ROPE_DIM]
    out = jnp.concatenate([x1 * cos - x2 * sin, x2 * cos + x1 * sin, tf[..., ROPE_DIM:]], axis=-1)
    return out.astype(t.dtype)


def strided_band_attention(q, k, v, window, dilation):
    B, H, Sp, Dh = q.shape
    n_back = window // dilation
    blk = n_back
    Q = Sp // dilation
    nb = Q // blk

    def to_blocks(t):
        t = t.reshape(B, H, Q, dilation, Dh).transpose(0, 1, 3, 2, 4)
        return t.reshape(B, H, dilation, nb, blk, Dh)

    def with_prev(t):
        prev = jnp.pad(t, ((0, 0), (0, 0), (0, 0), (1, 0), (0, 0), (0, 0)))[:, :, :, :-1]
        return jnp.concatenate([prev, t], axis=4)

    qb = to_blocks(q).astype(jnp.float32)
    kc = with_prev(to_blocks(k)).astype(jnp.float32)
    vc = with_prev(to_blocks(v)).astype(jnp.float32)
    s = jnp.einsum('bhrnqd,bhrnkd->bhrnqk', qb, kc) * (Dh ** -0.5)
    qi = jnp.arange(blk)[:, None]
    kj = jnp.arange(2 * blk)[None, :]
    delta = qi + blk - kj
    band = (delta >= 0) & (delta <= n_back)
    not_before_start = (jnp.arange(nb)[:, None, None] > 0) | (kj >= blk)[None]
    valid = band[None] & not_before_start
    s = jnp.where(valid, s, -jnp.inf)
    m = jnp.max(s, axis=-1, keepdims=True)
    p = jnp.exp(s - m)
    l = jnp.sum(p, axis=-1)
    o = jnp.einsum('bhrnqk,bhrnkd->bhrnqd', p, vc) / l[..., None]
    lse = m[..., 0] + jnp.log(l)
    o = o.reshape(B, H, dilation, Q, Dh).transpose(0, 1, 3, 2, 4).reshape(B, H, Sp, Dh)
    lse = lse.reshape(B, H, dilation, Q).transpose(0, 1, 3, 2).reshape(B, H, Sp)
    return o, lse


def dilated_attention(q, k, v):
    S = q.shape[2]
    s_pad = -(-S // SEQ_ALIGN) * SEQ_ALIGN
    pad = ((0, 0), (0, 0), (0, s_pad - S), (0, 0))
    qp, kp, vp = jnp.pad(q, pad), jnp.pad(k, pad), jnp.pad(v, pad)
    outs, lses = [], []
    for window, dilation in DILATED_PATTERNS:
        o, lse = strided_band_attention(qp, kp, vp, window, dilation)
        outs.append(o)
        lses.append(lse)
    wts = jax.nn.softmax(jnp.stack(lses, axis=0), axis=0)
    o = jnp.einsum('pbhs,pbhsd->bhsd', wts, jnp.stack(outs, axis=0))
    return o[:, :, :S]


def token_shift(y):
    return jnp.pad(y, ((0, 0), (1, 0), (0, 0)))[:, :-1]


def rwkv7_recurrence(r, w, k, v, a_vec, b_vec):
    Bsz, _, H, N = r.shape

    def step(state, inp):
        r_t, w_t, k_t, v_t, a_t, b_t = inp
        sa = jnp.einsum('bhvk,bhk->bhv', state, a_t)
        state = (state * w_t[:, :, None, :] + sa[..., None] * b_t[:, :, None, :]
                 + v_t[..., None] * k_t[:, :, None, :])
        return state, jnp.einsum('bhvk,bhk->bhv', state, r_t)

    xs = tuple(jnp.moveaxis(t.astype(jnp.float32), 1, 0) for t in (r, w, k, v, a_vec, b_vec))
    init = jnp.zeros((Bsz, H, N, N), jnp.float32)
    _, o = lax.scan(step, init, xs)
    return jnp.moveaxis(o, 0, 1)


def setup_inputs(seed: int = 0) -> dict:
    key = jax.random.key(seed)
    ks = jax.random.split(key, 24)

    def nrm(k, shape, scale):
        return jax.random.normal(k, shape, jnp.float32) * scale

    x = nrm(ks[0], (BATCH, SEQ, D_MODEL), 1.0)
    offsets = jax.random.randint(ks[1], (BATCH, 1), 0, 4096, dtype=jnp.int32)
    positions = (offsets + jnp.arange(SEQ, dtype=jnp.int32)[None, :]).astype(jnp.int32)
    col_scale = jnp.ones((IN_COLS,), jnp.float32)
    col_scale = col_scale.at[2 * ATT_WIDTH:3 * ATT_WIDTH].set(BETA)
    col_scale = col_scale.at[ATT_COLS + 2 * RWKV_WIDTH:ATT_COLS + 3 * RWKV_WIDTH].set(BETA)
    w_in = nrm(ks[2], (DEPTH, D_MODEL, IN_COLS), D_MODEL ** -0.5) * col_scale
    w_out = nrm(ks[3], (DEPTH, MIX_WIDTH, D_MODEL), MIX_WIDTH ** -0.5 * BETA)
    mu_shift = jax.random.uniform(ks[4], (DEPTH, SHIFT_COLS), jnp.float32, 0.2, 0.8)
    w0 = jax.random.uniform(ks[5], (DEPTH, RWKV_WIDTH), jnp.float32, -3.0, 3.0)
    w_decay_up = nrm(ks[6], (DEPTH, DECAY_LORA, RWKV_WIDTH), 0.5 * DECAY_LORA ** -0.5)
    a0 = nrm(ks[7], (DEPTH, RWKV_WIDTH), 0.1)
    w_aaa_up = nrm(ks[8], (DEPTH, AAA_LORA, RWKV_WIDTH), 0.5 * AAA_LORA ** -0.5)
    w_gate_up = nrm(ks[9], (DEPTH, GATE_LORA, RWKV_WIDTH), GATE_LORA ** -0.5)
    k_k = 0.85 + nrm(ks[10], (DEPTH, RWKV_WIDTH), 0.05)
    k_a = 1.0 + nrm(ks[11], (DEPTH, RWKV_WIDTH), 0.05)
    r_k = nrm(ks[12], (DEPTH, N_RWKV_HEADS, HEAD_DIM), 0.1)
    gn_g = 1.0 + nrm(ks[13], (DEPTH, RWKV_WIDTH), 0.02)
    gn_b = nrm(ks[14], (DEPTH, RWKV_WIDTH), 0.02)
    ln_mix_g = 1.0 + nrm(ks[15], (DEPTH, D_MODEL), 0.02)
    ln_mix_b = nrm(ks[16], (DEPTH, D_MODEL), 0.02)
    w_ffn_gate = nrm(ks[17], (DEPTH, D_MODEL, FFN_HIDDEN), D_MODEL ** -0.5)
    w_ffn_up = nrm(ks[18], (DEPTH, D_MODEL, FFN_HIDDEN), D_MODEL ** -0.5)
    w_ffn_down = nrm(ks[19], (DEPTH, FFN_HIDDEN, D_MODEL), FFN_HIDDEN ** -0.5 * BETA)
    ln_ffn_g = 1.0 + nrm(ks[20], (DEPTH, D_MODEL), 0.02)
    ln_ffn_b = nrm(ks[21], (DEPTH, D_MODEL), 0.02)
    return {'x': x, 'positions': positions, 'w_in': w_in, 'w_out': w_out,
            'mu_shift': mu_shift, 'w0': w0, 'w_decay_up': w_decay_up, 'a0': a0,
            'w_aaa_up': w_aaa_up, 'w_gate_up': w_gate_up, 'k_k': k_k, 'k_a': k_a,
            'r_k': r_k, 'gn_g': gn_g, 'gn_b': gn_b, 'ln_mix_g': ln_mix_g,
            'ln_mix_b': ln_mix_b, 'w_ffn_gate': w_ffn_gate, 'w_ffn_up': w_ffn_up,
            'w_ffn_down': w_ffn_down, 'ln_ffn_g': ln_ffn_g, 'ln_ffn_b': ln_ffn_b}


def reference(x, positions, w_in, w_out, mu_shift, w0, w_decay_up, a0, w_aaa_up,
              w_gate_up, k_k, k_a, r_k, gn_g, gn_b, ln_mix_g, ln_mix_b,
              w_ffn_gate, w_ffn_up, w_ffn_down, ln_ffn_g, ln_ffn_b):
    B, S, _ = x.shape
    rw_split = [RWKV_WIDTH, 2 * RWKV_WIDTH, 3 * RWKV_WIDTH,
                3 * RWKV_WIDTH + DECAY_LORA, 3 * RWKV_WIDTH + DECAY_LORA + AAA_LORA]
    h = x
    for l in range(DEPTH):
        y = jnp.einsum('bsd,dc->bsc', h, w_in[l])
        y_att, y_rw = y[..., :ATT_COLS], y[..., ATT_COLS:]

        q, k, v = jnp.split(y_att, 3, axis=-1)

        def att_heads(t):
            return t.reshape(B, S, N_ATT_HEADS, HEAD_DIM).transpose(0, 2, 1, 3)

        q = partial_rope(att_heads(q), positions)
        k = partial_rope(att_heads(k), positions)
        o_att = dilated_attention(q, k, att_heads(v))
        o_att = o_att.transpose(0, 2, 1, 3).reshape(B, S, ATT_WIDTH).astype(h.dtype)

        y_rw = y_rw + mu_shift[l] * (token_shift(y_rw) - y_rw)
        r, kr, vr, dw, da, dg = jnp.split(y_rw, rw_split, axis=-1)
        w_logit = (w0[l] + jnp.tanh(dw) @ w_decay_up[l]).astype(jnp.float32)
        decay = jnp.exp(-DECAY_SCALE * jax.nn.sigmoid(w_logit))
        a = jax.nn.sigmoid((a0[l] + da @ w_aaa_up[l]).astype(jnp.float32))
        g = jax.nn.sigmoid(dg) @ w_gate_up[l]

        def rw_heads(t):
            return t.reshape(B, S, N_RWKV_HEADS, HEAD_DIM)

        kk = rw_heads((kr * k_k[l]).astype(jnp.float32))
        kk = kk / jnp.maximum(jnp.linalg.norm(kk, axis=-1, keepdims=True), L2_EPS)
        a_h = rw_heads(a)
        k_a_h = k_a[l].reshape(N_RWKV_HEADS, HEAD_DIM).astype(jnp.float32)
        k_mod = rw_heads(kr).astype(jnp.float32) * (1.0 + (a_h - 1.0) * k_a_h)
        r_h = rw_heads(r).astype(jnp.float32)
        v_h = rw_heads(vr).astype(jnp.float32)
        o = rwkv7_recurrence(r_h, rw_heads(decay), k_mod, v_h, -kk, kk * a_h)
        mu = jnp.mean(o, axis=-1, keepdims=True)
        var = jnp.mean(jnp.square(o - mu), axis=-1, keepdims=True)
        o = ((o - mu) * lax.rsqrt(var + GN_EPS)).reshape(B, S, RWKV_WIDTH) * gn_g[l] + gn_b[l]
        bonus = jnp.sum(r_h * k_mod * r_k[l], axis=-1, keepdims=True) * v_h
        o_rw = ((o + bonus.reshape(B, S, RWKV_WIDTH)) * g).astype(h.dtype)

        mix = jnp.concatenate([o_att, o_rw], axis=-1) @ w_out[l]
        h = layer_norm(ALPHA * h + mix, ln_mix_g[l], ln_mix_b[l])

        ffn = (jax.nn.silu(h @ w_ffn_gate[l]) * (h @ w_ffn_up[l])) @ w_ffn_down[l]
        h = layer_norm(ALPHA * h + ffn, ln_ffn_g[l], ln_ffn_b[l])
    return h
```

```python
import functools
import math

import jax
import jax.numpy as jnp
from jax import lax
from jax.experimental import pallas as pl
from jax.experimental.pallas import tpu as pltpu

F32 = jnp.float32
BF16 = jnp.bfloat16

HEAD_DIM = 64
N_ATT_HEADS = 8
N_RWKV_HEADS = 8
ATT_WIDTH = N_ATT_HEADS * HEAD_DIM
RWKV_WIDTH = N_RWKV_HEADS * HEAD_DIM
ROPE_HALF = HEAD_DIM // 8
ROPE_THETA = 500000.0
DILATED_PATTERNS = ((128, 1), (512, 4), (2048, 16))
BAND = 128
DECAY_LORA = 64
AAA_LORA = 64
GATE_LORA = 128
ATT_COLS = 3 * ATT_WIDTH
SHIFT_COLS = 3 * RWKV_WIDTH + DECAY_LORA + AAA_LORA + GATE_LORA
LN_EPS = 1e-5
GN_EPS = 64e-5
L2_EPS = 1e-6
DECAY_SCALE = math.exp(-0.5)
NEG_BIG = -1e30

LANES = 128
CHUNK = 64
VMEM_LIMIT = 56 * 1024 * 1024

INV_FREQ = tuple(float(ROPE_THETA ** (-(j / ROPE_HALF))) for j in range(ROPE_HALF))


def _const_spec(shape):
    nd = len(shape)
    return pl.BlockSpec(shape, lambda *_: (0,) * nd, pipeline_mode=pl.Buffered(1))


def _rope_table_kernel(pos_ref, cos_ref, sin_ref):
    pos = pos_ref[...].astype(F32)
    for j in range(ROPE_HALF):
        ang = pos * INV_FREQ[j]
        cos_ref[j] = jnp.cos(ang)
        sin_ref[j] = jnp.sin(ang)


def _rope_tables(positions):
    n = positions.size
    pos2d = positions.reshape(n // LANES, LANES)
    cos, sin = pl.pallas_call(
        _rope_table_kernel,
        out_shape=(jax.ShapeDtypeStruct((ROPE_HALF, n // LANES, LANES), F32),) * 2,
        name="rope_tables",
    )(pos2d)
    cos_t = cos.reshape(ROPE_HALF, n).T
    sin_t = sin.reshape(ROPE_HALF, n).T
    ones = jnp.ones((n, HEAD_DIM - 2 * ROPE_HALF), F32)
    zeros = jnp.zeros_like(ones)
    zeros8 = jnp.zeros((n, ROPE_HALF), F32)
    c64 = jnp.concatenate([cos_t, cos_t, ones], axis=1)
    sa64 = jnp.concatenate([sin_t, zeros8, zeros], axis=1)
    sb64 = jnp.concatenate([zeros8, sin_t, zeros], axis=1)
    rep = LANES // HEAD_DIM
    return tuple(jnp.tile(t, (1, rep)) for t in (c64, sa64, sb64))


def _inproj_kernel(x_ref, w_ref, c_ref, sa_ref, sb_ref, q_ref, k_ref, v_ref, rw_ref):
    xb = x_ref[...].astype(BF16)
    c = c_ref[...]
    sa = sa_ref[...]
    sb = sb_ref[...]

    def proj(lo, hi):
        return jnp.dot(xb, w_ref[:, lo:hi], preferred_element_type=F32)

    def rope(t, scale):
        outs = []
        for j in range(ATT_WIDTH // LANES):
            blk = t[:, j * LANES:(j + 1) * LANES]
            up = pltpu.roll(blk, LANES - ROPE_HALF, 1)
            dn = pltpu.roll(blk, ROPE_HALF, 1)
            outs.append((blk * c - up * sa + dn * sb) * scale)
        return jnp.concatenate(outs, axis=1)

    q_ref[...] = rope(proj(0, ATT_WIDTH), HEAD_DIM ** -0.5).astype(BF16)
    k_ref[...] = rope(proj(ATT_WIDTH, 2 * ATT_WIDTH), 1.0).astype(BF16)
    v_ref[...] = proj(2 * ATT_WIDTH, ATT_COLS).astype(BF16)
    rw_ref[...] = proj(ATT_COLS, ATT_COLS + SHIFT_COLS)


def _inproj(x2d, w_in_bf16, tables, tile):
    n, d_model = x2d.shape
    in_cols = w_in_bf16.shape[1]
    row = lambda i: (i, 0)
    return pl.pallas_call(
        _inproj_kernel,
        grid=(n // tile,),
        in_specs=[
            pl.BlockSpec((tile, d_model), row),
            _const_spec((d_model, in_cols)),
            pl.BlockSpec((tile, LANES), row),
            pl.BlockSpec((tile, LANES), row),
            pl.BlockSpec((tile, LANES), row),
        ],
        out_specs=[
            pl.BlockSpec((tile, ATT_WIDTH), row),
            pl.BlockSpec((tile, ATT_WIDTH), row),
            pl.BlockSpec((tile, ATT_WIDTH), row),
            pl.BlockSpec((tile, SHIFT_COLS), row),
        ],
        out_shape=[
            jax.ShapeDtypeStruct((n, ATT_WIDTH), BF16),
            jax.ShapeDtypeStruct((n, ATT_WIDTH), BF16),
            jax.ShapeDtypeStruct((n, ATT_WIDTH), BF16),
            jax.ShapeDtypeStruct((n, SHIFT_COLS), F32),
        ],
        compiler_params=pltpu.CompilerParams(
            dimension_semantics=("arbitrary",), vmem_limit_bytes=VMEM_LIMIT),
        name="inproj",
    )(x2d, w_in_bf16, *tables)


def _attn_kernel(q_ref, kc_ref, kp_ref, vc_ref, vp_ref, o_ref, lse_ref, kbuf, vbuf, *, rq):
    i = pl.program_id(2)
    kbuf[0:BAND, :] = kp_ref[0]
    kbuf[BAND:, :] = kc_ref[0]
    vbuf[0:BAND, :] = vp_ref[0]
    vbuf[BAND:, :] = vc_ref[0]

    qi = lax.broadcasted_iota(jnp.int32, (BAND, 2 * BAND), 0)
    kj = lax.broadcasted_iota(jnp.int32, (BAND, 2 * BAND), 1)
    band_bias = jnp.where((kj >= qi) & (kj <= qi + BAND), 0.0, NEG_BIG).astype(F32)
    first_bias = jnp.where((kj >= BAND) | (i > 0), band_bias, NEG_BIG)

    for j in range(rq // BAND):
        bias = first_bias if j == 0 else band_bias
        rows = slice(j * BAND, (j + 1) * BAND)
        krows = slice(j * BAND, (j + 2) * BAND)
        for h in range(N_ATT_HEADS):
            cols = slice(h * HEAD_DIM, (h + 1) * HEAD_DIM)
            s = lax.dot_general(q_ref[0, rows, cols], kbuf[krows, cols],
                                (((1,), (1,)), ((), ())), preferred_element_type=F32)
            s = s + bias
            m = jnp.max(s, axis=1, keepdims=True)
            p = jnp.exp(s - m)
            l = jnp.sum(p, axis=1, keepdims=True)
            o = jnp.dot(p.astype(BF16), vbuf[krows, cols], preferred_element_type=F32)
            o_ref[0, rows, cols] = o * (1.0 / l)
            lse_ref[0, rows, cols] = jnp.broadcast_to(m + jnp.log(l), (BAND, HEAD_DIM))


def _band_attention(q, k, v, dilation, rq):
    b, s, w = q.shape
    sd = s // dilation
    view = lambda t: t.reshape(b, sd, dilation * w)
    sub = rq // BAND
    cur = lambda bi, r, i: (bi, i, r)
    prev = lambda bi, r, i: (bi, jnp.maximum(i * sub - 1, 0), r)
    o, lse = pl.pallas_call(
        functools.partial(_attn_kernel, rq=rq),
        grid=(b, dilation, sd // rq),
        in_specs=[
            pl.BlockSpec((1, rq, w), cur),
            pl.BlockSpec((1, rq, w), cur),
            pl.BlockSpec((1, BAND, w), prev),
            pl.BlockSpec((1, rq, w), cur),
            pl.BlockSpec((1, BAND, w), prev),
        ],
        out_specs=[pl.BlockSpec((1, rq, w), cur), pl.BlockSpec((1, rq, w), cur)],
        out_shape=[jax.ShapeDtypeStruct((b, sd, dilation * w), F32)] * 2,
        scratch_shapes=[pltpu.VMEM((rq + BAND, w), BF16), pltpu.VMEM((rq + BAND, w), BF16)],
        compiler_params=pltpu.CompilerParams(
            dimension_semantics=("arbitrary",) * 3, vmem_limit_bytes=VMEM_LIMIT),
        name=f"band_attn_d{dilation}",
    )(view(q), view(k), view(k), view(v), view(v))
    return o.reshape(b, s, w), lse.reshape(b, s, w)


def _split2(x):
    hi = x.astype(BF16)
    lo = (x - hi.astype(F32)).astype(BF16)
    return hi, lo


def _mm3(a, b, dims=(((1,), (0,)), ((), ()))):
    ah, al = _split2(a)
    bh, bl = _split2(b)
    dg = functools.partial(lax.dot_general, dimension_numbers=dims, preferred_element_type=F32)
    return dg(ah, bh) + (dg(ah, bl) + dg(al, bh))


_NT = (((1,), (1,)), ((), ()))
_TN = (((0,), (0,)), ((), ()))


def _seg_sum(x, seg):
    hi, lo = _split2(x)
    return (jnp.dot(hi, seg, preferred_element_type=F32)
            + jnp.dot(lo, seg, preferred_element_type=F32))


def _rwkv_kernel(y_ref, mu_ref, w0_ref, wdec_ref, a0_ref, waaa_ref, wgate_ref, kk_ref, ka_ref,
                 rk_ref, gng_ref, gnb_ref, seg_ref, o_ref,
                 state, carry, r_s, k_s, v_s, a_s, b_s, ld_s, obuf, *, tile):
    t = pl.program_id(1)

    @pl.when(t == 0)
    def _():
        state[...] = jnp.zeros_like(state)
        carry[...] = jnp.zeros_like(carry)

    seg = seg_ref[...]
    y = y_ref[0]
    rows_i = lax.broadcasted_iota(jnp.int32, y.shape, 0)
    prev = jnp.where(rows_i == 0, carry[...], pltpu.roll(y, 1, 0))
    carry[...] = y[tile - 1:tile, :]
    ys = y + mu_ref[...] * (prev - y)

    w = RWKV_WIDTH
    r = ys[:, 0:w]
    kr = ys[:, w:2 * w]
    vr = ys[:, 2 * w:3 * w]
    dw = ys[:, 3 * w:3 * w + DECAY_LORA]
    da = ys[:, 3 * w + DECAY_LORA:3 * w + DECAY_LORA + AAA_LORA]
    dg = ys[:, 3 * w + DECAY_LORA + AAA_LORA:]

    w_logit = w0_ref[...] + jnp.dot(jnp.tanh(dw).astype(BF16), wdec_ref[...],
                                    preferred_element_type=F32)
    ld_s[...] = -DECAY_SCALE * jax.nn.sigmoid(w_logit)
    a = jax.nn.sigmoid(a0_ref[...] + jnp.dot(da.astype(BF16), waaa_ref[...],
                                             preferred_element_type=F32))
    g = jnp.dot(jax.nn.sigmoid(dg).astype(BF16), wgate_ref[...], preferred_element_type=F32)

    kk = kr * kk_ref[...]
    norm = jnp.sqrt(_seg_sum(kk * kk, seg))
    kk = kk / jnp.maximum(norm, L2_EPS)
    k_mod = kr * (1.0 + (a - 1.0) * ka_ref[...])
    r_s[...] = r
    k_s[...] = k_mod
    v_s[...] = vr
    a_s[...] = -kk
    b_s[...] = kk * a

    ti = lax.broadcasted_iota(jnp.int32, (CHUNK, CHUNK), 0)
    si = lax.broadcasted_iota(jnp.int32, (CHUNK, CHUNK), 1)
    tri_incl = (si <= ti)
    tri_strict = (si < ti)
    tri_incl_b = tri_incl.astype(BF16)
    eye = (si == ti).astype(F32)
    level_masks = []
    for s in (1, 2, 4, 8, 16, 32):
        level_masks.append(((ti // (2 * s)) == (si // (2 * s)))
                           & ((ti % (2 * s)) >= s) & ((si % (2 * s)) < s))

    def chunk_body(c, carry_unused):
        rows = pl.ds(pl.multiple_of(c * CHUNK, CHUNK), CHUNK)
        ld = ld_s[rows, :]
        l_hi = ld.astype(BF16)
        l_r1 = ld - l_hi.astype(F32)
        l_mid = l_r1.astype(BF16)
        l_lo = (l_r1 - l_mid.astype(F32)).astype(BF16)
        cum = (jnp.dot(tri_incl_b, l_hi, preferred_element_type=F32)
               + jnp.dot(tri_incl_b, l_mid, preferred_element_type=F32)
               + jnp.dot(tri_incl_b, l_lo, preferred_element_type=F32))
        p_in = jnp.exp(cum)
        p_ex = jnp.exp(cum - ld)
        ip = jnp.exp(-cum)
        at = a_s[rows, :] * p_ex
        rt = r_s[rows, :] * p_in
        bt = b_s[rows, :] * ip
        kt = k_s[rows, :] * ip
        vv = v_s[rows, :]
        p_end = p_in[CHUNK - 1:CHUNK, :]

        for h in range(N_RWKV_HEADS):
            cols = slice(h * HEAD_DIM, (h + 1) * HEAD_DIM)
            lhs = jnp.concatenate([at[:, cols], rt[:, cols]], axis=0)
            bh = bt[:, cols]
            kh = kt[:, cols]
            vh = vv[:, cols]
            mb = _mm3(lhs, bh, _NT)
            mk = _mm3(lhs, kh, _NT)
            n_ab = jnp.where(tri_strict, mb[0:CHUNK], 0.0)
            a_ak = jnp.where(tri_strict, mk[0:CHUNK], 0.0)
            a_rb = jnp.where(tri_incl, mb[CHUNK:], 0.0)
            a_rk = jnp.where(tri_incl, mk[CHUNK:], 0.0)
            tinv = eye + jnp.where(level_masks[0], n_ab, 0.0)
            for lvl in range(1, 6):
                off = jnp.where(level_masks[lvl], n_ab, 0.0)
                tinv = tinv + _mm3(tinv, _mm3(off, tinv))
            s0 = state[h]
            w0 = _mm3(lhs, s0, _NT)
            u = _mm3(tinv, w0[0:CHUNK] + _mm3(a_ak, vh))
            obuf[rows, cols] = w0[CHUNK:] + _mm3(a_rb, u) + _mm3(a_rk, vh)
            s_new = s0 + _mm3(u, bh, _TN) + _mm3(vh, kh, _TN)
            state[h] = s_new * p_end[:, cols]
        return carry_unused

    lax.fori_loop(0, tile // CHUNK, chunk_body, 0)

    o = obuf[...]
    inv_n = 1.0 / HEAD_DIM
    mean = _seg_sum(o, seg) * inv_n
    d = o - mean
    var = _seg_sum(d * d, seg) * inv_n
    on = d * lax.rsqrt(var + GN_EPS) * gng_ref[...] + gnb_ref[...]
    rr = r_s[...]
    bonus = _seg_sum(rr * k_s[...] * rk_ref[...], seg) * v_s[...]
    o_ref[0] = ((on + bonus) * g).astype(BF16)


def _rwkv(y_rw, p, tile):
    b, s, cols = y_rw.shape
    w = RWKV_WIDTH
    row = lambda t: t.reshape(1, -1).astype(F32)
    lanes = jnp.arange(w) // HEAD_DIM
    seg = (lanes[:, None] == lanes[None, :]).astype(BF16)
    vec = lambda n: _const_spec((1, n))
    scr = lambda: pltpu.VMEM((tile, w), F32)
    return pl.pallas_call(
        functools.partial(_rwkv_kernel, tile=tile),
        grid=(b, s // tile),
        in_specs=[
            pl.BlockSpec((1, tile, cols), lambda bi, t: (bi, t, 0)),
            vec(cols), vec(w), _const_spec((DECAY_LORA, w)), vec(w), _const_spec((AAA_LORA, w)),
            _const_spec((GATE_LORA, w)), vec(w), vec(w), vec(w), vec(w), vec(w),
            _const_spec((w, w)),
        ],
        out_specs=pl.BlockSpec((1, tile, w), lambda bi, t: (bi, t, 0)),
        out_shape=jax.ShapeDtypeStruct((b, s, w), BF16),
        scratch_shapes=[
            pltpu.VMEM((N_RWKV_HEADS, HEAD_DIM, HEAD_DIM), F32),
            pltpu.VMEM((1, cols), F32),
            scr(), scr(), scr(), scr(), scr(), scr(), scr(),
        ],
        compiler_params=pltpu.CompilerParams(
            dimension_semantics=("arbitrary", "arbitrary"), vmem_limit_bytes=VMEM_LIMIT),
        name="rwkv7",
    )(y_rw, row(p["mu_shift"]), row(p["w0"]), p["w_decay_up"].astype(BF16), row(p["a0"]),
      p["w_aaa_up"].astype(BF16), p["w_gate_up"].astype(BF16), row(p["k_k"]), row(p["k_a"]),
      row(p["r_k"]), row(p["gn_g"]), row(p["gn_b"]), seg)


def _layer_norm(x, g, b):
    mu = jnp.mean(x, axis=-1, keepdims=True)
    d = x - mu
    var = jnp.mean(d * d, axis=-1, keepdims=True)
    return d * lax.rsqrt(var + LN_EPS) * g + b


def _tail_kernel(o1, l1, o2, l2, o3, l3, orw_ref, x_ref, wout_ref, g1_ref, b1_ref,
                 wg_ref, wu_ref, wd_ref, g2_ref, b2_ref, out_ref, *, alpha):
    la, lb, lc = l1[...], l2[...], l3[...]
    m = jnp.maximum(jnp.maximum(la, lb), lc)
    ea, eb, ec = jnp.exp(la - m), jnp.exp(lb - m), jnp.exp(lc - m)
    o_att = (ea * o1[...] + eb * o2[...] + ec * o3[...]) / (ea + eb + ec)
    mix = (jnp.dot(o_att.astype(BF16), wout_ref[0:ATT_WIDTH, :], preferred_element_type=F32)
           + jnp.dot(orw_ref[...], wout_ref[ATT_WIDTH:, :], preferred_element_type=F32))
    h = _layer_norm(alpha * x_ref[...] + mix, g1_ref[...], b1_ref[...])
    hb = h.astype(BF16)
    gate = jnp.dot(hb, wg_ref[...], preferred_element_type=F32)
    up = jnp.dot(hb, wu_ref[...], preferred_element_type=F32)
    act = (gate * jax.nn.sigmoid(gate) * up).astype(BF16)
    ffn = jnp.dot(act, wd_ref[...], preferred_element_type=F32)
    out_ref[...] = _layer_norm(alpha * h + ffn, g2_ref[...], b2_ref[...])


def _tail(att, o_rw, x2d, p, alpha, tile):
    n, d_model = x2d.shape
    hidden = p["w_ffn_gate"].shape[1]
    row = lambda i: (i, 0)
    vec = lambda t: t.reshape(1, -1).astype(F32)
    att_specs = [pl.BlockSpec((tile, ATT_WIDTH), row)] * 6
    return pl.pallas_call(
        functools.partial(_tail_kernel, alpha=alpha),
        grid=(n // tile,),
        in_specs=att_specs + [
            pl.BlockSpec((tile, RWKV_WIDTH), row),
            pl.BlockSpec((tile, d_model), row),
            _const_spec((ATT_WIDTH + RWKV_WIDTH, d_model)),
            _const_spec((1, d_model)), _const_spec((1, d_model)),
            _const_spec((d_model, hidden)), _const_spec((d_model, hidden)),
            _const_spec((hidden, d_model)),
            _const_spec((1, d_model)), _const_spec((1, d_model)),
        ],
        out_specs=pl.BlockSpec((tile, d_model), row),
        out_shape=jax.ShapeDtypeStruct((n, d_model), F32),
        compiler_params=pltpu.CompilerParams(
            dimension_semantics=("arbitrary",), vmem_limit_bytes=VMEM_LIMIT),
        name="merge_outproj_ffn",
    )(*att, o_rw, x2d, p["w_out"].astype(BF16), vec(p["ln_mix_g"]), vec(p["ln_mix_b"]),
      p["w_ffn_gate"].astype(BF16), p["w_ffn_up"].astype(BF16), p["w_ffn_down"].astype(BF16),
      vec(p["ln_ffn_g"]), vec(p["ln_ffn_b"]))


def kernel(x, positions, w_in, w_out, mu_shift, w0, w_decay_up, a0, w_aaa_up, w_gate_up, k_k, k_a,
           r_k, gn_g, gn_b, ln_mix_g, ln_mix_b, w_ffn_gate, w_ffn_up, w_ffn_down, ln_ffn_g,
           ln_ffn_b):
    b, s, d_model = x.shape
    depth = w_in.shape[0]
    alpha = (2.0 * depth) ** 0.25
    assert s % 2048 == 0 and d_model == 2 * ATT_WIDTH
    params = dict(w_in=w_in, w_out=w_out, mu_shift=mu_shift, w0=w0, w_decay_up=w_decay_up, a0=a0,
                  w_aaa_up=w_aaa_up, w_gate_up=w_gate_up, k_k=k_k, k_a=k_a, r_k=r_k, gn_g=gn_g,
                  gn_b=gn_b, ln_mix_g=ln_mix_g, ln_mix_b=ln_mix_b, w_ffn_gate=w_ffn_gate,
                  w_ffn_up=w_ffn_up, w_ffn_down=w_ffn_down, ln_ffn_g=ln_ffn_g, ln_ffn_b=ln_ffn_b)
    tables = _rope_tables(positions)
    h2d = x.reshape(b * s, d_model)
    for layer in range(depth):
        p = {name: t[layer] for name, t in params.items()}
        q, k, v, y_rw = _inproj(h2d, p["w_in"].astype(BF16), tables, tile=512)
        shape3 = lambda t: t.reshape(b, s, t.shape[-1])
        att = []
        for _, dilation in DILATED_PATTERNS:
            o, lse = _band_attention(shape3(q), shape3(k), shape3(v), dilation, rq=512)
            att += [o.reshape(b * s, ATT_WIDTH), lse.reshape(b * s, ATT_WIDTH)]
        o_rw = _rwkv(shape3(y_rw), p, tile=256).reshape(b * s, RWKV_WIDTH)
        h2d = _tail(att, o_rw, h2d, p, alpha, tile=256)
    return h2d.reshape(b, s, d_model)
```

```python
import functools
import math

import jax
import jax.numpy as jnp
from jax import lax
from jax.experimental import pallas as pl
from jax.experimental.pallas import tpu as pltpu

F32 = jnp.float32
BF16 = jnp.bfloat16

HEAD_DIM = 64
N_ATT_HEADS = 8
N_RWKV_HEADS = 8
ATT_WIDTH = N_ATT_HEADS * HEAD_DIM
RWKV_WIDTH = N_RWKV_HEADS * HEAD_DIM
ROPE_HALF = HEAD_DIM // 8
ROPE_THETA = 500000.0
DILATED_PATTERNS = ((128, 1), (512, 4), (2048, 16))
BAND = 128
DECAY_LORA = 64
AAA_LORA = 64
GATE_LORA = 128
ATT_COLS = 3 * ATT_WIDTH
SHIFT_COLS = 3 * RWKV_WIDTH + DECAY_LORA + AAA_LORA + GATE_LORA
LN_EPS = 1e-5
GN_EPS = 64e-5
L2_EPS = 1e-6
DECAY_SCALE = math.exp(-0.5)
NEG_BIG = -1e30

LANES = 128
CHUNK = 64
VMEM_LIMIT = 56 * 1024 * 1024

INV_FREQ = tuple(float(ROPE_THETA ** (-(j / ROPE_HALF))) for j in range(ROPE_HALF))


def _const_spec(shape):
    nd = len(shape)
    return pl.BlockSpec(shape, lambda *_: (0,) * nd, pipeline_mode=pl.Buffered(1))


def _rope_table_kernel(pos_ref, cos_ref, sin_ref):
    pos = pos_ref[...].astype(F32)
    for j in range(ROPE_HALF):
        ang = pos * INV_FREQ[j]
        cos_ref[j] = jnp.cos(ang)
        sin_ref[j] = jnp.sin(ang)


def _rope_tables(positions):
    n = positions.size
    pos2d = positions.reshape(n // LANES, LANES)
    cos, sin = pl.pallas_call(
        _rope_table_kernel,
        out_shape=(jax.ShapeDtypeStruct((ROPE_HALF, n // LANES, LANES), F32),) * 2,
        name="rope_tables",
    )(pos2d)
    cos_t = cos.reshape(ROPE_HALF, n).T
    sin_t = sin.reshape(ROPE_HALF, n).T
    ones = jnp.ones((n, HEAD_DIM - 2 * ROPE_HALF), F32)
    zeros = jnp.zeros_like(ones)
    zeros8 = jnp.zeros((n, ROPE_HALF), F32)
    c64 = jnp.concatenate([cos_t, cos_t, ones], axis=1)
    sa64 = jnp.concatenate([sin_t, zeros8, zeros], axis=1)
    sb64 = jnp.concatenate([zeros8, sin_t, zeros], axis=1)
    rep = LANES // HEAD_DIM
    return tuple(jnp.tile(t, (1, rep)) for t in (c64, sa64, sb64))


def _inproj_kernel(x_ref, w_ref, c_ref, sa_ref, sb_ref, q_ref, k_ref, v_ref, rw_ref):
    xb = x_ref[...].astype(BF16)
    c = c_ref[...]
    sa = sa_ref[...]
    sb = sb_ref[...]

    def proj(lo, hi):
        return jnp.dot(xb, w_ref[:, lo:hi], preferred_element_type=F32)

    def rope(t, scale):
        outs = []
        for j in range(ATT_WIDTH // LANES):
            blk = t[:, j * LANES:(j + 1) * LANES]
            up = pltpu.roll(blk, LANES - ROPE_HALF, 1)
            dn = pltpu.roll(blk, ROPE_HALF, 1)
            outs.append((blk * c - up * sa + dn * sb) * scale)
        return jnp.concatenate(outs, axis=1)

    q_ref[...] = rope(proj(0, ATT_WIDTH), HEAD_DIM ** -0.5).astype(BF16)
    k_ref[...] = rope(proj(ATT_WIDTH, 2 * ATT_WIDTH), 1.0).astype(BF16)
    v_ref[...] = proj(2 * ATT_WIDTH, ATT_COLS).astype(BF16)
    rw_ref[...] = proj(ATT_COLS, ATT_COLS + SHIFT_COLS)


def _inproj(x2d, w_in_bf16, tables, tile):
    n, d_model = x2d.shape
    in_cols = w_in_bf16.shape[1]
    row = lambda i: (i, 0)
    return pl.pallas_call(
        _inproj_kernel,
        grid=(n // tile,),
        in_specs=[
            pl.BlockSpec((tile, d_model), row),
            _const_spec((d_model, in_cols)),
            pl.BlockSpec((tile, LANES), row),
            pl.BlockSpec((tile, LANES), row),
            pl.BlockSpec((tile, LANES), row),
        ],
        out_specs=[
            pl.BlockSpec((tile, ATT_WIDTH), row),
            pl.BlockSpec((tile, ATT_WIDTH), row),
            pl.BlockSpec((tile, ATT_WIDTH), row),
            pl.BlockSpec((tile, SHIFT_COLS), row),
        ],
        out_shape=[
            jax.ShapeDtypeStruct((n, ATT_WIDTH), BF16),
            jax.ShapeDtypeStruct((n, ATT_WIDTH), BF16),
            jax.ShapeDtypeStruct((n, ATT_WIDTH), BF16),
            jax.ShapeDtypeStruct((n, SHIFT_COLS), F32),
        ],
        compiler_params=pltpu.CompilerParams(
            dimension_semantics=("arbitrary",), vmem_limit_bytes=VMEM_LIMIT),
        name="inproj",
    )(x2d, w_in_bf16, *tables)


def _attn_kernel(q_ref, kc_ref, kp_ref, vc_ref, vp_ref, o_ref, lse_ref, kbuf, vbuf, *, rq):
    i = pl.program_id(2)
    kbuf[0:BAND, :] = kp_ref[0]
    kbuf[BAND:, :] = kc_ref[0]
    vbuf[0:BAND, :] = vp_ref[0]
    vbuf[BAND:, :] = vc_ref[0]

    qi = lax.broadcasted_iota(jnp.int32, (BAND, 2 * BAND), 0)
    kj = lax.broadcasted_iota(jnp.int32, (BAND, 2 * BAND), 1)
    band_bias = jnp.where((kj >= qi) & (kj <= qi + BAND), 0.0, NEG_BIG).astype(F32)
    first_bias = jnp.where((kj >= BAND) | (i > 0), band_bias, NEG_BIG)

    for j in range(rq // BAND):
        bias = first_bias if j == 0 else band_bias
        rows = slice(j * BAND, (j + 1) * BAND)
        krows = slice(j * BAND, (j + 2) * BAND)
        for h in range(N_ATT_HEADS):
            cols = slice(h * HEAD_DIM, (h + 1) * HEAD_DIM)
            s = lax.dot_general(q_ref[0, rows, cols], kbuf[krows, cols],
                                (((1,), (1,)), ((), ())), preferred_element_type=F32)
            s = s + bias
            m = jnp.max(s, axis=1, keepdims=True)
            p = jnp.exp(s - m)
            l = jnp.sum(p, axis=1, keepdims=True)
            o = jnp.dot(p.astype(BF16), vbuf[krows, cols], preferred_element_type=F32)
            o_ref[0, rows, cols] = o * (1.0 / l)
            lse_ref[0, rows, cols] = jnp.broadcast_to(m + jnp.log(l), (BAND, HEAD_DIM))


def _band_attention(q, k, v, dilation, rq):
    b, s, w = q.shape
    sd = s // dilation
    view = lambda t: t.reshape(b, sd, dilation * w)
    sub = rq // BAND
    cur = lambda bi, r, i: (bi, i, r)
    prev = lambda bi, r, i: (bi, jnp.maximum(i * sub - 1, 0), r)
    o, lse = pl.pallas_call(
        functools.partial(_attn_kernel, rq=rq),
        grid=(b, dilation, sd // rq),
        in_specs=[
            pl.BlockSpec((1, rq, w), cur),
            pl.BlockSpec((1, rq, w), cur),
            pl.BlockSpec((1, BAND, w), prev),
            pl.BlockSpec((1, rq, w), cur),
            pl.BlockSpec((1, BAND, w), prev),
        ],
        out_specs=[pl.BlockSpec((1, rq, w), cur), pl.BlockSpec((1, rq, w), cur)],
        out_shape=[jax.ShapeDtypeStruct((b, sd, dilation * w), F32)] * 2,
        scratch_shapes=[pltpu.VMEM((rq + BAND, w), BF16), pltpu.VMEM((rq + BAND, w), BF16)],
        compiler_params=pltpu.CompilerParams(
            dimension_semantics=("arbitrary",) * 3, vmem_limit_bytes=VMEM_LIMIT),
        name=f"band_attn_d{dilation}",
    )(view(q), view(k), view(k), view(v), view(v))
    return o.reshape(b, s, w), lse.reshape(b, s, w)


PAIR = 2 * HEAD_DIM
N_PAIRS = RWKV_WIDTH // PAIR

_NN = (((1,), (0,)), ((), ()))
_NT = (((1,), (1,)), ((), ()))
_TN = (((0,), (0,)), ((), ()))

PIECES_SCORES = 1
PIECES_INVERSE = 1
PIECES_APPLY = 1
PIECES_CHUNK_OPS = 1
PIECES_SCAN = 2


def _split(x, pieces):
    parts = []
    rem = x
    for i in range(pieces):
        part = rem.astype(BF16)
        parts.append(part)
        if i + 1 < pieces:
            rem = rem - part.astype(F32)
    return parts


def _dot(a_parts, b_parts, dims):
    order = max(len(a_parts), len(b_parts))
    acc = None
    for i, a in enumerate(a_parts):
        for j, b in enumerate(b_parts):
            if i + j < order:
                term = lax.dot_general(a, b, dims, preferred_element_type=F32)
                acc = term if acc is None else acc + term
    return acc


def _bd2(x):
    lane = lax.broadcasted_iota(jnp.int32, x.shape, 1)
    zero = jnp.zeros_like(x)
    return jnp.concatenate([jnp.where(lane < HEAD_DIM, x, zero),
                            jnp.where(lane >= HEAD_DIM, x, zero)], axis=0)


def _pair_nn(a, b, pieces):
    return _dot(_split(a, pieces), [_bd2(p) for p in _split(b, pieces)], _NN)


def _pair_nt(a, b, pieces):
    return _dot(_split(a, pieces), [_bd2(p) for p in _split(b, pieces)], _NT)


def _pair_tn(a, b, pieces):
    full = _dot(_split(a, pieces), _split(b, pieces), _TN)
    lane = lax.broadcasted_iota(jnp.int32, (HEAD_DIM, PAIR), 1)
    return jnp.where(lane < HEAD_DIM, full[0:HEAD_DIM], full[HEAD_DIM:])


def _seg_sum(x, seg):
    return _dot(_split(x, 2), [seg], _NN)


def _rwkv_kernel(y_ref, mu_ref, w0_ref, wdec_ref, a0_ref, waaa_ref, wgate_ref, kk_ref, ka_ref,
                 rk_ref, gng_ref, gnb_ref, seg_ref, o_ref, state, carry, *, tile):
    t = pl.program_id(1)

    @pl.when(t == 0)
    def _():
        state[...] = jnp.zeros_like(state)
        carry[...] = jnp.zeros_like(carry)

    seg = seg_ref[...]
    y = y_ref[0]
    rows_i = lax.broadcasted_iota(jnp.int32, y.shape, 0)
    prev = jnp.where(rows_i == 0, carry[...], pltpu.roll(y, 1, 0))
    carry[...] = y[tile - 1:tile, :]
    ys = y + mu_ref[...] * (prev - y)

    w = RWKV_WIDTH
    r = ys[:, 0:w]
    kr = ys[:, w:2 * w]
    vr = ys[:, 2 * w:3 * w]
    dw = ys[:, 3 * w:3 * w + DECAY_LORA]
    da = ys[:, 3 * w + DECAY_LORA:3 * w + DECAY_LORA + AAA_LORA]
    dg = ys[:, 3 * w + DECAY_LORA + AAA_LORA:]

    w_logit = w0_ref[...] + jnp.dot(jnp.tanh(dw).astype(BF16), wdec_ref[...],
                                    preferred_element_type=F32)
    ld = -DECAY_SCALE * jax.nn.sigmoid(w_logit)
    a = jax.nn.sigmoid(a0_ref[...] + jnp.dot(da.astype(BF16), waaa_ref[...],
                                             preferred_element_type=F32))
    g = jnp.dot(jax.nn.sigmoid(dg).astype(BF16), wgate_ref[...], preferred_element_type=F32)

    kk = kr * kk_ref[...]
    kk = kk / jnp.maximum(jnp.sqrt(_seg_sum(kk * kk, seg)), L2_EPS)
    k_mod = kr * (1.0 + (a - 1.0) * ka_ref[...])
    a_vec = -kk
    b_vec = kk * a

    n_chunks = tile // CHUNK
    ti = lax.broadcasted_iota(jnp.int32, (tile, tile), 0)
    si = lax.broadcasted_iota(jnp.int32, (tile, tile), 1)
    tri = ((ti // CHUNK == si // CHUNK) & (si <= ti)).astype(BF16)
    cum = _dot([tri], _split(ld, 3), _NN)
    cum_end = jnp.concatenate(
        [jnp.broadcast_to(cum[(c + 1) * CHUNK - 1:(c + 1) * CHUNK, :], (CHUNK, w))
         for c in range(n_chunks)], axis=0)
    p_in = jnp.exp(cum)
    inv_p = jnp.exp(-cum)
    to_end = jnp.exp(cum_end - cum)
    at = a_vec * jnp.exp(cum - ld)
    rt = r * p_in
    bt = b_vec * inv_p
    kt = k_mod * inv_p
    b_end = b_vec * to_end
    k_end = k_mod * to_end

    row = lax.broadcasted_iota(jnp.int32, (CHUNK, PAIR), 0)
    col = lax.broadcasted_iota(jnp.int32, (CHUNK, PAIR), 1) % HEAD_DIM
    strict = col < row
    incl = col <= row
    eye = (col == row).astype(F32)
    level_masks = [((row // (2 * s)) == (col // (2 * s))) & ((row % (2 * s)) >= s) & ((col % (2 * s)) < s)
                   for s in (1, 2, 4, 8, 16, 32)]

    probs = [(c, hp) for c in range(n_chunks) for hp in range(N_PAIRS)]

    def blk(x, prob):
        c, hp = prob
        return x[c * CHUNK:(c + 1) * CHUNK, hp * PAIR:(hp + 1) * PAIR]

    def masked(mask, x):
        return jnp.where(mask, x, 0.0)

    lhs = [jnp.concatenate([blk(at, pr), blk(rt, pr)], axis=0) for pr in probs]
    mb = [_pair_nt(l, blk(bt, pr), PIECES_SCORES) for l, pr in zip(lhs, probs)]
    mk = [_pair_nt(l, blk(kt, pr), PIECES_SCORES) for l, pr in zip(lhs, probs)]
    n_ab = [masked(strict, m[0:CHUNK]) for m in mb]
    a_ak = [masked(strict, m[0:CHUNK]) for m in mk]
    a_rb = [masked(incl, m[CHUNK:]) for m in mb]
    a_rk = [masked(incl, m[CHUNK:]) for m in mk]

    tinv = [eye + masked(level_masks[0], n) for n in n_ab]
    for lvl in range(1, len(level_masks)):
        x = [_pair_nn(masked(level_masks[lvl], n), ti_, PIECES_INVERSE) for n, ti_ in zip(n_ab, tinv)]
        tinv = [ti_ + _pair_nn(ti_, x_, PIECES_INVERSE) for ti_, x_ in zip(tinv, x)]

    a_bar = [_pair_nn(ti_, blk(at, pr), PIECES_APPLY) for ti_, pr in zip(tinv, probs)]
    av = [_pair_nn(m, blk(vr, pr), PIECES_APPLY) for m, pr in zip(a_ak, probs)]
    x1 = [_pair_nn(ti_, v_, PIECES_APPLY) for ti_, v_ in zip(tinv, av)]
    r_bar = [blk(rt, pr) + _pair_nn(m, ab, PIECES_APPLY) for m, ab, pr in zip(a_rb, a_bar, probs)]
    o1 = [_pair_nn(mb_, x_, PIECES_APPLY) + _pair_nn(mk_, blk(vr, pr), PIECES_APPLY)
          for mb_, mk_, x_, pr in zip(a_rb, a_rk, x1, probs)]
    g_mat = [eye * blk(p_in, pr)[CHUNK - 1:CHUNK, :] + _pair_tn(ab, blk(b_end, pr), PIECES_CHUNK_OPS)
             for ab, pr in zip(a_bar, probs)]
    z_mat = [_pair_tn(jnp.concatenate([x_, blk(vr, pr)], axis=0),
                      jnp.concatenate([blk(b_end, pr), blk(k_end, pr)], axis=0), PIECES_CHUNK_OPS)
             for x_, pr in zip(x1, probs)]

    s_cur = [state[hp] for hp in range(N_PAIRS)]
    o_rows = []
    for c in range(n_chunks):
        o_c = []
        for hp in range(N_PAIRS):
            i = c * N_PAIRS + hp
            o_c.append(_pair_nt(r_bar[i], s_cur[hp], PIECES_SCAN) + o1[i])
        s_cur = [_pair_nn(s_cur[hp], g_mat[c * N_PAIRS + hp], PIECES_SCAN) + z_mat[c * N_PAIRS + hp]
                 for hp in range(N_PAIRS)]
        o_rows.append(jnp.concatenate(o_c, axis=1))
    for hp in range(N_PAIRS):
        state[hp] = s_cur[hp]
    o = jnp.concatenate(o_rows, axis=0)

    inv_n = 1.0 / HEAD_DIM
    mean = _seg_sum(o, seg) * inv_n
    d = o - mean
    var = _seg_sum(d * d, seg) * inv_n
    on = d * lax.rsqrt(var + GN_EPS) * gng_ref[...] + gnb_ref[...]
    bonus = _seg_sum(r * k_mod * rk_ref[...], seg) * vr
    o_ref[0] = ((on + bonus) * g).astype(BF16)


def _rwkv(y_rw, p, tile):
    b, s, cols = y_rw.shape
    w = RWKV_WIDTH
    row = lambda t: t.reshape(1, -1).astype(F32)
    lanes = jnp.arange(w) // HEAD_DIM
    seg = (lanes[:, None] == lanes[None, :]).astype(BF16)
    vec = lambda n: _const_spec((1, n))
    return pl.pallas_call(
        functools.partial(_rwkv_kernel, tile=tile),
        grid=(b, s // tile),
        in_specs=[
            pl.BlockSpec((1, tile, cols), lambda bi, t: (bi, t, 0)),
            vec(cols), vec(w), _const_spec((DECAY_LORA, w)), vec(w), _const_spec((AAA_LORA, w)),
            _const_spec((GATE_LORA, w)), vec(w), vec(w), vec(w), vec(w), vec(w),
            _const_spec((w, w)),
        ],
        out_specs=pl.BlockSpec((1, tile, w), lambda bi, t: (bi, t, 0)),
        out_shape=jax.ShapeDtypeStruct((b, s, w), BF16),
        scratch_shapes=[
            pltpu.VMEM((N_PAIRS, HEAD_DIM, PAIR), F32),
            pltpu.VMEM((1, cols), F32),
        ],
        compiler_params=pltpu.CompilerParams(
            dimension_semantics=("arbitrary", "arbitrary"), vmem_limit_bytes=VMEM_LIMIT),
        name="rwkv7",
    )(y_rw, row(p["mu_shift"]), row(p["w0"]), p["w_decay_up"].astype(BF16), row(p["a0"]),
      p["w_aaa_up"].astype(BF16), p["w_gate_up"].astype(BF16), row(p["k_k"]), row(p["k_a"]),
      row(p["r_k"]), row(p["gn_g"]), row(p["gn_b"]), seg)


def _layer_norm(x, g, b):
    mu = jnp.mean(x, axis=-1, keepdims=True)
    d = x - mu
    var = jnp.mean(d * d, axis=-1, keepdims=True)
    return d * lax.rsqrt(var + LN_EPS) * g + b


def _tail_kernel(o1, l1, o2, l2, o3, l3, orw_ref, x_ref, wout_ref, g1_ref, b1_ref,
                 wg_ref, wu_ref, wd_ref, g2_ref, b2_ref, out_ref, *, alpha):
    la, lb, lc = l1[...], l2[...], l3[...]
    m = jnp.maximum(jnp.maximum(la, lb), lc)
    ea, eb, ec = jnp.exp(la - m), jnp.exp(lb - m), jnp.exp(lc - m)
    o_att = (ea * o1[...] + eb * o2[...] + ec * o3[...]) / (ea + eb + ec)
    mix = (jnp.dot(o_att.astype(BF16), wout_ref[0:ATT_WIDTH, :], preferred_element_type=F32)
           + jnp.dot(orw_ref[...], wout_ref[ATT_WIDTH:, :], preferred_element_type=F32))
    h = _layer_norm(alpha * x_ref[...] + mix, g1_ref[...], b1_ref[...])
    hb = h.astype(BF16)
    gate = jnp.dot(hb, wg_ref[...], preferred_element_type=F32)
    up = jnp.dot(hb, wu_ref[...], preferred_element_type=F32)
    act = (gate * jax.nn.sigmoid(gate) * up).astype(BF16)
    ffn = jnp.dot(act, wd_ref[...], preferred_element_type=F32)
    out_ref[...] = _layer_norm(alpha * h + ffn, g2_ref[...], b2_ref[...])


def _tail(att, o_rw, x2d, p, alpha, tile):
    n, d_model = x2d.shape
    hidden = p["w_ffn_gate"].shape[1]
    row = lambda i: (i, 0)
    vec = lambda t: t.reshape(1, -1).astype(F32)
    att_specs = [pl.BlockSpec((tile, ATT_WIDTH), row)] * 6
    return pl.pallas_call(
        functools.partial(_tail_kernel, alpha=alpha),
        grid=(n // tile,),
        in_specs=att_specs + [
            pl.BlockSpec((tile, RWKV_WIDTH), row),
            pl.BlockSpec((tile, d_model), row),
            _const_spec((ATT_WIDTH + RWKV_WIDTH, d_model)),
            _const_spec((1, d_model)), _const_spec((1, d_model)),
            _const_spec((d_model, hidden)), _const_spec((d_model, hidden)),
            _const_spec((hidden, d_model)),
            _const_spec((1, d_model)), _const_spec((1, d_model)),
        ],
        out_specs=pl.BlockSpec((tile, d_model), row),
        out_shape=jax.ShapeDtypeStruct((n, d_model), F32),
        compiler_params=pltpu.CompilerParams(
            dimension_semantics=("arbitrary",), vmem_limit_bytes=VMEM_LIMIT),
        name="merge_outproj_ffn",
    )(*att, o_rw, x2d, p["w_out"].astype(BF16), vec(p["ln_mix_g"]), vec(p["ln_mix_b"]),
      p["w_ffn_gate"].astype(BF16), p["w_ffn_up"].astype(BF16), p["w_ffn_down"].astype(BF16),
      vec(p["ln_ffn_g"]), vec(p["ln_ffn_b"]))


def kernel(x, positions, w_in, w_out, mu_shift, w0, w_decay_up, a0, w_aaa_up, w_gate_up, k_k, k_a,
           r_k, gn_g, gn_b, ln_mix_g, ln_mix_b, w_ffn_gate, w_ffn_up, w_ffn_down, ln_ffn_g,
           ln_ffn_b):
    b, s, d_model = x.shape
    depth = w_in.shape[0]
    alpha = (2.0 * depth) ** 0.25
    assert s % 2048 == 0 and d_model == 2 * ATT_WIDTH
    params = dict(w_in=w_in, w_out=w_out, mu_shift=mu_shift, w0=w0, w_decay_up=w_decay_up, a0=a0,
                  w_aaa_up=w_aaa_up, w_gate_up=w_gate_up, k_k=k_k, k_a=k_a, r_k=r_k, gn_g=gn_g,
                  gn_b=gn_b, ln_mix_g=ln_mix_g, ln_mix_b=ln_mix_b, w_ffn_gate=w_ffn_gate,
                  w_ffn_up=w_ffn_up, w_ffn_down=w_ffn_down, ln_ffn_g=ln_ffn_g, ln_ffn_b=ln_ffn_b)
    tables = _rope_tables(positions)
    h2d = x.reshape(b * s, d_model)
    for layer in range(depth):
        p = {name: t[layer] for name, t in params.items()}
        q, k, v, y_rw = _inproj(h2d, p["w_in"].astype(BF16), tables, tile=512)
        shape3 = lambda t: t.reshape(b, s, t.shape[-1])
        att = []
        for _, dilation in DILATED_PATTERNS:
            o, lse = _band_attention(shape3(q), shape3(k), shape3(v), dilation, rq=512)
            att += [o.reshape(b * s, ATT_WIDTH), lse.reshape(b * s, ATT_WIDTH)]
        o_rw = _rwkv(shape3(y_rw), p, tile=256).reshape(b * s, RWKV_WIDTH)
        h2d = _tail(att, o_rw, h2d, p, alpha, tile=256)
    return h2d.reshape(b, s, d_model)
```

```python
import functools
import math

import jax
import jax.numpy as jnp
from jax import lax
from jax.experimental import pallas as pl
from jax.experimental.pallas import tpu as pltpu

F32 = jnp.float32
BF16 = jnp.bfloat16

HEAD_DIM = 64
N_ATT_HEADS = 8
N_RWKV_HEADS = 8
ATT_WIDTH = N_ATT_HEADS * HEAD_DIM
RWKV_WIDTH = N_RWKV_HEADS * HEAD_DIM
ROPE_HALF = HEAD_DIM // 8
ROPE_THETA = 500000.0
DILATED_PATTERNS = ((128, 1), (512, 4), (2048, 16))
BAND = 128
DECAY_LORA = 64
AAA_LORA = 64
GATE_LORA = 128
ATT_COLS = 3 * ATT_WIDTH
SHIFT_COLS = 3 * RWKV_WIDTH + DECAY_LORA + AAA_LORA + GATE_LORA
LN_EPS = 1e-5
GN_EPS = 64e-5
L2_EPS = 1e-6
DECAY_SCALE = math.exp(-0.5)
NEG_BIG = -1e30

LANES = 128
CHUNK = 64
VMEM_LIMIT = 56 * 1024 * 1024

INV_FREQ = tuple(float(ROPE_THETA ** (-(j / ROPE_HALF))) for j in range(ROPE_HALF))


def _const_spec(shape):
    nd = len(shape)
    return pl.BlockSpec(shape, lambda *_: (0,) * nd, pipeline_mode=pl.Buffered(1))


def _rope_table_kernel(pos_ref, cos_ref, sin_ref):
    pos = pos_ref[...].astype(F32)
    for j in range(ROPE_HALF):
        ang = pos * INV_FREQ[j]
        cos_ref[j] = jnp.cos(ang)
        sin_ref[j] = jnp.sin(ang)


def _rope_tables(positions):
    n = positions.size
    pos2d = positions.reshape(n // LANES, LANES)
    cos, sin = pl.pallas_call(
        _rope_table_kernel,
        out_shape=(jax.ShapeDtypeStruct((ROPE_HALF, n // LANES, LANES), F32),) * 2,
        name="rope_tables",
    )(pos2d)
    cos_t = cos.reshape(ROPE_HALF, n).T
    sin_t = sin.reshape(ROPE_HALF, n).T
    ones = jnp.ones((n, HEAD_DIM - 2 * ROPE_HALF), F32)
    zeros = jnp.zeros_like(ones)
    zeros8 = jnp.zeros((n, ROPE_HALF), F32)
    c64 = jnp.concatenate([cos_t, cos_t, ones], axis=1)
    sa64 = jnp.concatenate([sin_t, zeros8, zeros], axis=1)
    sb64 = jnp.concatenate([zeros8, sin_t, zeros], axis=1)
    rep = LANES // HEAD_DIM
    return tuple(jnp.tile(t, (1, rep)) for t in (c64, sa64, sb64))


def _inproj_kernel(x_ref, w_ref, c_ref, sa_ref, sb_ref, *refs, tile):
    out_refs, rw_ref, stage = refs[:9], refs[9], refs[10]
    xb = x_ref[0].astype(BF16)
    c = c_ref[0]
    sa = sa_ref[0]
    sb = sb_ref[0]
    n_blk = ATT_WIDTH // LANES

    def proj(lo, hi):
        return jnp.dot(xb, w_ref[:, lo:hi], preferred_element_type=F32)

    def rope(t, scale):
        outs = []
        for j in range(n_blk):
            blk = t[:, j * LANES:(j + 1) * LANES]
            up = pltpu.roll(blk, LANES - ROPE_HALF, 1)
            dn = pltpu.roll(blk, ROPE_HALF, 1)
            outs.append((blk * c - up * sa + dn * sb) * scale)
        return outs

    def emit(blocks, group):
        for j, blk in enumerate(blocks):
            stage[j] = blk
            out_refs[group][0, :, j * LANES:(j + 1) * LANES] = blk.astype(BF16)
        for pi, (_, d) in enumerate(DILATED_PATTERNS[1:], start=1):
            rows = tile // d
            for r in range(d):
                for j in range(n_blk):
                    cls = stage[j, pl.ds(r, rows, stride=d), :]
                    out_refs[3 * pi + group][0, r, :, j * LANES:(j + 1) * LANES] = cls.astype(BF16)

    def lane_blocks(t):
        return [t[:, j * LANES:(j + 1) * LANES] for j in range(n_blk)]

    emit(rope(proj(0, ATT_WIDTH), HEAD_DIM ** -0.5), 0)
    emit(rope(proj(ATT_WIDTH, 2 * ATT_WIDTH), 1.0), 1)
    emit(lane_blocks(proj(2 * ATT_WIDTH, ATT_COLS)), 2)
    rw_ref[0] = proj(ATT_COLS, ATT_COLS + SHIFT_COLS)


def _inproj(x, w_in_bf16, tables, tile):
    b, s, d_model = x.shape
    in_cols = w_in_bf16.shape[1]
    row = lambda bi, i: (bi, i, 0)
    cls = lambda bi, i: (bi, 0, i, 0)
    qkv_specs, qkv_shapes = [], []
    for _, d in DILATED_PATTERNS:
        for _ in range(3):
            if d == 1:
                qkv_specs.append(pl.BlockSpec((1, tile, ATT_WIDTH), row))
                qkv_shapes.append(jax.ShapeDtypeStruct((b, s, ATT_WIDTH), BF16))
            else:
                qkv_specs.append(pl.BlockSpec((1, d, tile // d, ATT_WIDTH), cls))
                qkv_shapes.append(jax.ShapeDtypeStruct((b, d, s // d, ATT_WIDTH), BF16))
    outs = pl.pallas_call(
        functools.partial(_inproj_kernel, tile=tile),
        grid=(b, s // tile),
        in_specs=[
            pl.BlockSpec((1, tile, d_model), row),
            _const_spec((d_model, in_cols)),
            pl.BlockSpec((1, tile, LANES), row),
            pl.BlockSpec((1, tile, LANES), row),
            pl.BlockSpec((1, tile, LANES), row),
        ],
        out_specs=qkv_specs + [pl.BlockSpec((1, tile, SHIFT_COLS), row)],
        out_shape=qkv_shapes + [jax.ShapeDtypeStruct((b, s, SHIFT_COLS), F32)],
        scratch_shapes=[pltpu.VMEM((ATT_WIDTH // LANES, tile, LANES), F32)],
        compiler_params=pltpu.CompilerParams(
            dimension_semantics=("arbitrary", "arbitrary"), vmem_limit_bytes=VMEM_LIMIT),
        name="inproj",
    )(x, w_in_bf16, *tables)
    return outs[:9], outs[9]


SUPER = 2048
ATT_GROUP = 4
KV_ROWS = max(d * (BAND + SUPER // d) for _, d in DILATED_PATTERNS)


def _attn_kernel(*refs):
    ins, o_ref = refs[:15], refs[15]
    k_lo, k_hi, v_lo, v_hi, o_acc, lse_acc = refs[16:]
    sb = pl.program_id(1)

    qi = lax.broadcasted_iota(jnp.int32, (BAND, 2 * BAND), 0)
    kj = lax.broadcasted_iota(jnp.int32, (BAND, 2 * BAND), 1)
    band_bias = jnp.where((kj >= qi) & (kj <= qi + BAND), 0.0, NEG_BIG).astype(F32)
    first_bias = jnp.where((kj >= BAND) | (sb > 0), band_bias, NEG_BIG)
    lane_lo = lax.broadcasted_iota(jnp.int32, (BAND, LANES), 1) < HEAD_DIM

    def stage(dst_lo, dst_hi, rows, val):
        lane = lax.broadcasted_iota(jnp.int32, val.shape, 1)
        zero = jnp.zeros_like(val)
        dst_lo[rows, :] = jnp.where(lane < HEAD_DIM, val, zero)
        dst_hi[rows, :] = jnp.where(lane >= HEAD_DIM, val, zero)

    for pi, (_, d) in enumerate(DILATED_PATTERNS):
        q_ref, kc_ref, kp_ref, vc_ref, vp_ref = ins[5 * pi:5 * pi + 5]
        n = SUPER // d
        span = BAND + n
        for r in range(d):
            idx = (0,) if d == 1 else (0, r)
            base = r * span
            stage(k_lo, k_hi, slice(base, base + BAND), kp_ref[idx])
            stage(k_lo, k_hi, slice(base + BAND, base + span), kc_ref[idx])
            stage(v_lo, v_hi, slice(base, base + BAND), vp_ref[idx])
            stage(v_lo, v_hi, slice(base + BAND, base + span), vc_ref[idx])

        units = [(r, j) for j in range(n // BAND) for r in range(d)]
        for g0 in range(0, len(units), ATT_GROUP):
            group = units[g0:g0 + ATT_GROUP]
            scores = []
            for r, j in group:
                idx = (0,) if d == 1 else (0, r)
                q = q_ref[idx + (slice(j * BAND, (j + 1) * BAND), slice(None))]
                krows = slice(r * span + j * BAND, r * span + (j + 2) * BAND)
                k_bd = jnp.concatenate([k_lo[krows, :], k_hi[krows, :]], axis=0)
                scores.append(lax.dot_general(q, k_bd, _NT_DIMS, preferred_element_type=F32))
            probs, stats = [], []
            for (r, j), s in zip(group, scores):
                bias = first_bias if j == 0 else band_bias
                s0 = s[:, :2 * BAND] + bias
                s1 = s[:, 2 * BAND:] + bias
                m0 = jnp.max(s0, axis=1, keepdims=True)
                m1 = jnp.max(s1, axis=1, keepdims=True)
                p0 = jnp.exp(s0 - m0)
                p1 = jnp.exp(s1 - m1)
                l0 = jnp.sum(p0, axis=1, keepdims=True)
                l1 = jnp.sum(p1, axis=1, keepdims=True)
                probs.append(jnp.concatenate([p0, p1], axis=1).astype(BF16))
                stats.append((m0, m1, l0, l1))
            for (r, j), p, (m0, m1, l0, l1) in zip(group, probs, stats):
                krows = slice(r * span + j * BAND, r * span + (j + 2) * BAND)
                v_bd = jnp.concatenate([v_lo[krows, :], v_hi[krows, :]], axis=0)
                o = jnp.dot(p, v_bd, preferred_element_type=F32)
                inv_l = jnp.where(lane_lo, 1.0 / l0, 1.0 / l1)
                lse = jnp.where(lane_lo, m0 + jnp.log(l0), m1 + jnp.log(l1))
                if d == 1:
                    rows = slice(j * BAND, (j + 1) * BAND)
                else:
                    rows = pl.ds(j * BAND * d + r, BAND, stride=d)
                o_acc[pi, rows, :] = o * inv_l
                lse_acc[pi, rows, :] = lse

    la, lb, lc = lse_acc[0], lse_acc[1], lse_acc[2]
    m = jnp.maximum(jnp.maximum(la, lb), lc)
    ea, eb, ec = jnp.exp(la - m), jnp.exp(lb - m), jnp.exp(lc - m)
    o_ref[0] = ((ea * o_acc[0] + eb * o_acc[1] + ec * o_acc[2]) / (ea + eb + ec)).astype(BF16)


_NT_DIMS = (((1,), (1,)), ((), ()))


def _dilated_attention(qkv):
    b, s, w = qkv[0].shape
    args, specs = [], []
    for pi, (_, d) in enumerate(DILATED_PATTERNS):
        q, k, v = qkv[3 * pi:3 * pi + 3]
        n = SUPER // d
        sub = n // BAND
        if d == 1:
            cur = pl.BlockSpec((1, n, LANES), lambda bi, sb, hp: (bi, sb, hp))
            prev = pl.BlockSpec((1, BAND, LANES),
                                lambda bi, sb, hp, sub=sub: (bi, jnp.maximum(sb * sub - 1, 0), hp))
        else:
            cur = pl.BlockSpec((1, d, n, LANES), lambda bi, sb, hp: (bi, 0, sb, hp))
            prev = pl.BlockSpec((1, d, BAND, LANES),
                                lambda bi, sb, hp, sub=sub: (bi, 0, jnp.maximum(sb * sub - 1, 0), hp))
        args += [q, k, k, v, v]
        specs += [cur, cur, prev, cur, prev]
    kv_scratch = lambda: pltpu.VMEM((KV_ROWS, LANES), BF16)
    acc_scratch = lambda: pltpu.VMEM((len(DILATED_PATTERNS), SUPER, LANES), F32)
    return pl.pallas_call(
        _attn_kernel,
        grid=(b, s // SUPER, w // LANES),
        in_specs=specs,
        out_specs=pl.BlockSpec((1, SUPER, LANES), lambda bi, sb, hp: (bi, sb, hp)),
        out_shape=jax.ShapeDtypeStruct((b, s, w), BF16),
        scratch_shapes=[kv_scratch(), kv_scratch(), kv_scratch(), kv_scratch(),
                        acc_scratch(), acc_scratch()],
        compiler_params=pltpu.CompilerParams(
            dimension_semantics=("arbitrary",) * 3, vmem_limit_bytes=VMEM_LIMIT),
        name="dilated_attn",
    )(*args)


PAIR = 2 * HEAD_DIM
N_PAIRS = RWKV_WIDTH // PAIR

_NN = (((1,), (0,)), ((), ()))
_NT = (((1,), (1,)), ((), ()))
_TN = (((0,), (0,)), ((), ()))

PIECES_SCORES = 1
PIECES_INVERSE = 1
PIECES_APPLY = 1
PIECES_CHUNK_OPS = 1
PIECES_SCAN = 2


def _split(x, pieces):
    parts = []
    rem = x
    for i in range(pieces):
        part = rem.astype(BF16)
        parts.append(part)
        if i + 1 < pieces:
            rem = rem - part.astype(F32)
    return parts


def _dot(a_parts, b_parts, dims):
    order = max(len(a_parts), len(b_parts))
    acc = None
    for i, a in enumerate(a_parts):
        for j, b in enumerate(b_parts):
            if i + j < order:
                term = lax.dot_general(a, b, dims, preferred_element_type=F32)
                acc = term if acc is None else acc + term
    return acc


def _bd2(x):
    lane = lax.broadcasted_iota(jnp.int32, x.shape, 1)
    zero = jnp.zeros_like(x)
    return jnp.concatenate([jnp.where(lane < HEAD_DIM, x, zero),
                            jnp.where(lane >= HEAD_DIM, x, zero)], axis=0)


def _pair_nn(a, b, pieces):
    return _dot(_split(a, pieces), [_bd2(p) for p in _split(b, pieces)], _NN)


def _pair_nt(a, b, pieces):
    return _dot(_split(a, pieces), [_bd2(p) for p in _split(b, pieces)], _NT)


def _pair_tn(a, b, pieces):
    full = _dot(_split(a, pieces), _split(b, pieces), _TN)
    lane = lax.broadcasted_iota(jnp.int32, (HEAD_DIM, PAIR), 1)
    return jnp.where(lane < HEAD_DIM, full[0:HEAD_DIM], full[HEAD_DIM:])


def _seg_sum(x, seg):
    return _dot(_split(x, 2), [seg], _NN)


def _rwkv_kernel(y_ref, mu_ref, w0_ref, wdec_ref, a0_ref, waaa_ref, wgate_ref, kk_ref, ka_ref,
                 rk_ref, gng_ref, gnb_ref, seg_ref, o_ref, state, carry, *, tile):
    t = pl.program_id(1)

    @pl.when(t == 0)
    def _():
        state[...] = jnp.zeros_like(state)
        carry[...] = jnp.zeros_like(carry)

    seg = seg_ref[...]
    y = y_ref[0]
    rows_i = lax.broadcasted_iota(jnp.int32, y.shape, 0)
    prev = jnp.where(rows_i == 0, carry[...], pltpu.roll(y, 1, 0))
    carry[...] = y[tile - 1:tile, :]
    ys = y + mu_ref[...] * (prev - y)

    w = RWKV_WIDTH
    r = ys[:, 0:w]
    kr = ys[:, w:2 * w]
    vr = ys[:, 2 * w:3 * w]
    dw = ys[:, 3 * w:3 * w + DECAY_LORA]
    da = ys[:, 3 * w + DECAY_LORA:3 * w + DECAY_LORA + AAA_LORA]
    dg = ys[:, 3 * w + DECAY_LORA + AAA_LORA:]

    w_logit = w0_ref[...] + jnp.dot(jnp.tanh(dw).astype(BF16), wdec_ref[...],
                                    preferred_element_type=F32)
    ld = -DECAY_SCALE * jax.nn.sigmoid(w_logit)
    a = jax.nn.sigmoid(a0_ref[...] + jnp.dot(da.astype(BF16), waaa_ref[...],
                                             preferred_element_type=F32))
    g = jnp.dot(jax.nn.sigmoid(dg).astype(BF16), wgate_ref[...], preferred_element_type=F32)

    kk = kr * kk_ref[...]
    kk = kk / jnp.maximum(jnp.sqrt(_seg_sum(kk * kk, seg)), L2_EPS)
    k_mod = kr * (1.0 + (a - 1.0) * ka_ref[...])
    a_vec = -kk
    b_vec = kk * a

    n_chunks = tile // CHUNK
    ti = lax.broadcasted_iota(jnp.int32, (tile, tile), 0)
    si = lax.broadcasted_iota(jnp.int32, (tile, tile), 1)
    tri = ((ti // CHUNK == si // CHUNK) & (si <= ti)).astype(BF16)
    cum = _dot([tri], _split(ld, 3), _NN)
    cum_end = jnp.concatenate(
        [jnp.broadcast_to(cum[(c + 1) * CHUNK - 1:(c + 1) * CHUNK, :], (CHUNK, w))
         for c in range(n_chunks)], axis=0)
    p_in = jnp.exp(cum)
    inv_p = jnp.exp(-cum)
    to_end = jnp.exp(cum_end - cum)
    at = a_vec * jnp.exp(cum - ld)
    rt = r * p_in
    bt = b_vec * inv_p
    kt = k_mod * inv_p
    b_end = b_vec * to_end
    k_end = k_mod * to_end

    row = lax.broadcasted_iota(jnp.int32, (CHUNK, PAIR), 0)
    col = lax.broadcasted_iota(jnp.int32, (CHUNK, PAIR), 1) % HEAD_DIM
    strict = col < row
    incl = col <= row
    eye = (col == row).astype(F32)
    level_masks = [((row // (2 * s)) == (col // (2 * s))) & ((row % (2 * s)) >= s) & ((col % (2 * s)) < s)
                   for s in (1, 2, 4, 8, 16, 32)]

    probs = [(c, hp) for c in range(n_chunks) for hp in range(N_PAIRS)]

    def blk(x, prob):
        c, hp = prob
        return x[c * CHUNK:(c + 1) * CHUNK, hp * PAIR:(hp + 1) * PAIR]

    def masked(mask, x):
        return jnp.where(mask, x, 0.0)

    lhs = [jnp.concatenate([blk(at, pr), blk(rt, pr)], axis=0) for pr in probs]
    mb = [_pair_nt(l, blk(bt, pr), PIECES_SCORES) for l, pr in zip(lhs, probs)]
    mk = [_pair_nt(l, blk(kt, pr), PIECES_SCORES) for l, pr in zip(lhs, probs)]
    n_ab = [masked(strict, m[0:CHUNK]) for m in mb]
    a_ak = [masked(strict, m[0:CHUNK]) for m in mk]
    a_rb = [masked(incl, m[CHUNK:]) for m in mb]
    a_rk = [masked(incl, m[CHUNK:]) for m in mk]

    tinv = [eye + masked(level_masks[0], n) for n in n_ab]
    for lvl in range(1, len(level_masks)):
        x = [_pair_nn(masked(level_masks[lvl], n), ti_, PIECES_INVERSE) for n, ti_ in zip(n_ab, tinv)]
        tinv = [ti_ + _pair_nn(ti_, x_, PIECES_INVERSE) for ti_, x_ in zip(tinv, x)]

    a_bar = [_pair_nn(ti_, blk(at, pr), PIECES_APPLY) for ti_, pr in zip(tinv, probs)]
    av = [_pair_nn(m, blk(vr, pr), PIECES_APPLY) for m, pr in zip(a_ak, probs)]
    x1 = [_pair_nn(ti_, v_, PIECES_APPLY) for ti_, v_ in zip(tinv, av)]
    r_bar = [blk(rt, pr) + _pair_nn(m, ab, PIECES_APPLY) for m, ab, pr in zip(a_rb, a_bar, probs)]
    o1 = [_pair_nn(mb_, x_, PIECES_APPLY) + _pair_nn(mk_, blk(vr, pr), PIECES_APPLY)
          for mb_, mk_, x_, pr in zip(a_rb, a_rk, x1, probs)]
    g_mat = [eye * blk(p_in, pr)[CHUNK - 1:CHUNK, :] + _pair_tn(ab, blk(b_end, pr), PIECES_CHUNK_OPS)
             for ab, pr in zip(a_bar, probs)]
    z_mat = [_pair_tn(jnp.concatenate([x_, blk(vr, pr)], axis=0),
                      jnp.concatenate([blk(b_end, pr), blk(k_end, pr)], axis=0), PIECES_CHUNK_OPS)
             for x_, pr in zip(x1, probs)]

    s_cur = [state[hp] for hp in range(N_PAIRS)]
    o_rows = []
    for c in range(n_chunks):
        o_c = []
        for hp in range(N_PAIRS):
            i = c * N_PAIRS + hp
            o_c.append(_pair_nt(r_bar[i], s_cur[hp], PIECES_SCAN) + o1[i])
        s_cur = [_pair_nn(s_cur[hp], g_mat[c * N_PAIRS + hp], PIECES_SCAN) + z_mat[c * N_PAIRS + hp]
                 for hp in range(N_PAIRS)]
        o_rows.append(jnp.concatenate(o_c, axis=1))
    for hp in range(N_PAIRS):
        state[hp] = s_cur[hp]
    o = jnp.concatenate(o_rows, axis=0)

    inv_n = 1.0 / HEAD_DIM
    mean = _seg_sum(o, seg) * inv_n
    d = o - mean
    var = _seg_sum(d * d, seg) * inv_n
    on = d * lax.rsqrt(var + GN_EPS) * gng_ref[...] + gnb_ref[...]
    bonus = _seg_sum(r * k_mod * rk_ref[...], seg) * vr
    o_ref[0] = ((on + bonus) * g).astype(BF16)


def _rwkv(y_rw, p, tile):
    b, s, cols = y_rw.shape
    w = RWKV_WIDTH
    row = lambda t: t.reshape(1, -1).astype(F32)
    lanes = jnp.arange(w) // HEAD_DIM
    seg = (lanes[:, None] == lanes[None, :]).astype(BF16)
    vec = lambda n: _const_spec((1, n))
    return pl.pallas_call(
        functools.partial(_rwkv_kernel, tile=tile),
        grid=(b, s // tile),
        in_specs=[
            pl.BlockSpec((1, tile, cols), lambda bi, t: (bi, t, 0)),
            vec(cols), vec(w), _const_spec((DECAY_LORA, w)), vec(w), _const_spec((AAA_LORA, w)),
            _const_spec((GATE_LORA, w)), vec(w), vec(w), vec(w), vec(w), vec(w),
            _const_spec((w, w)),
        ],
        out_specs=pl.BlockSpec((1, tile, w), lambda bi, t: (bi, t, 0)),
        out_shape=jax.ShapeDtypeStruct((b, s, w), BF16),
        scratch_shapes=[
            pltpu.VMEM((N_PAIRS, HEAD_DIM, PAIR), F32),
            pltpu.VMEM((1, cols), F32),
        ],
        compiler_params=pltpu.CompilerParams(
            dimension_semantics=("arbitrary", "arbitrary"), vmem_limit_bytes=VMEM_LIMIT),
        name="rwkv7",
    )(y_rw, row(p["mu_shift"]), row(p["w0"]), p["w_decay_up"].astype(BF16), row(p["a0"]),
      p["w_aaa_up"].astype(BF16), p["w_gate_up"].astype(BF16), row(p["k_k"]), row(p["k_a"]),
      row(p["r_k"]), row(p["gn_g"]), row(p["gn_b"]), seg)


def _layer_norm(x, g, b):
    mu = jnp.mean(x, axis=-1, keepdims=True)
    d = x - mu
    var = jnp.mean(d * d, axis=-1, keepdims=True)
    return d * lax.rsqrt(var + LN_EPS) * g + b


def _tail_kernel(oatt_ref, orw_ref, x_ref, wout_ref, g1_ref, b1_ref,
                 wg_ref, wu_ref, wd_ref, g2_ref, b2_ref, out_ref, *, alpha):
    mix = (jnp.dot(oatt_ref[...], wout_ref[0:ATT_WIDTH, :], preferred_element_type=F32)
           + jnp.dot(orw_ref[...], wout_ref[ATT_WIDTH:, :], preferred_element_type=F32))
    h = _layer_norm(alpha * x_ref[...] + mix, g1_ref[...], b1_ref[...])
    hb = h.astype(BF16)
    gate = jnp.dot(hb, wg_ref[...], preferred_element_type=F32)
    up = jnp.dot(hb, wu_ref[...], preferred_element_type=F32)
    act = (gate * jax.nn.sigmoid(gate) * up).astype(BF16)
    ffn = jnp.dot(act, wd_ref[...], preferred_element_type=F32)
    out_ref[...] = _layer_norm(alpha * h + ffn, g2_ref[...], b2_ref[...])


def _tail(o_att, o_rw, x2d, p, alpha, tile):
    n, d_model = x2d.shape
    hidden = p["w_ffn_gate"].shape[1]
    row = lambda i: (i, 0)
    vec = lambda t: t.reshape(1, -1).astype(F32)
    return pl.pallas_call(
        functools.partial(_tail_kernel, alpha=alpha),
        grid=(n // tile,),
        in_specs=[
            pl.BlockSpec((tile, ATT_WIDTH), row),
            pl.BlockSpec((tile, RWKV_WIDTH), row),
            pl.BlockSpec((tile, d_model), row),
            _const_spec((ATT_WIDTH + RWKV_WIDTH, d_model)),
            _const_spec((1, d_model)), _const_spec((1, d_model)),
            _const_spec((d_model, hidden)), _const_spec((d_model, hidden)),
            _const_spec((hidden, d_model)),
            _const_spec((1, d_model)), _const_spec((1, d_model)),
        ],
        out_specs=pl.BlockSpec((tile, d_model), row),
        out_shape=jax.ShapeDtypeStruct((n, d_model), F32),
        compiler_params=pltpu.CompilerParams(
            dimension_semantics=("arbitrary",), vmem_limit_bytes=VMEM_LIMIT),
        name="merge_outproj_ffn",
    )(o_att, o_rw, x2d, p["w_out"].astype(BF16), vec(p["ln_mix_g"]), vec(p["ln_mix_b"]),
      p["w_ffn_gate"].astype(BF16), p["w_ffn_up"].astype(BF16), p["w_ffn_down"].astype(BF16),
      vec(p["ln_ffn_g"]), vec(p["ln_ffn_b"]))


def kernel(x, positions, w_in, w_out, mu_shift, w0, w_decay_up, a0, w_aaa_up, w_gate_up, k_k, k_a,
           r_k, gn_g, gn_b, ln_mix_g, ln_mix_b, w_ffn_gate, w_ffn_up, w_ffn_down, ln_ffn_g,
           ln_ffn_b):
    b, s, d_model = x.shape
    depth = w_in.shape[0]
    alpha = (2.0 * depth) ** 0.25
    assert s % 2048 == 0 and d_model == 2 * ATT_WIDTH
    params = dict(w_in=w_in, w_out=w_out, mu_shift=mu_shift, w0=w0, w_decay_up=w_decay_up, a0=a0,
                  w_aaa_up=w_aaa_up, w_gate_up=w_gate_up, k_k=k_k, k_a=k_a, r_k=r_k, gn_g=gn_g,
                  gn_b=gn_b, ln_mix_g=ln_mix_g, ln_mix_b=ln_mix_b, w_ffn_gate=w_ffn_gate,
                  w_ffn_up=w_ffn_up, w_ffn_down=w_ffn_down, ln_ffn_g=ln_ffn_g, ln_ffn_b=ln_ffn_b)
    tables = tuple(t.reshape(b, s, LANES) for t in _rope_tables(positions))
    h2d = x.reshape(b * s, d_model)
    for layer in range(depth):
        p = {name: t[layer] for name, t in params.items()}
        qkv, y_rw = _inproj(h2d.reshape(b, s, d_model), p["w_in"].astype(BF16), tables, tile=512)
        o_att = _dilated_attention(qkv).reshape(b * s, ATT_WIDTH)
        o_rw = _rwkv(y_rw, p, tile=256).reshape(b * s, RWKV_WIDTH)
        h2d = _tail(o_att, o_rw, h2d, p, alpha, tile=256)
    return h2d.reshape(b, s, d_model)
```

```python
import functools
import math

import jax
import jax.numpy as jnp
from jax import lax
from jax.experimental import pallas as pl
from jax.experimental.pallas import tpu as pltpu

F32 = jnp.float32
BF16 = jnp.bfloat16

HEAD_DIM = 64
N_ATT_HEADS = 8
N_RWKV_HEADS = 8
ATT_WIDTH = N_ATT_HEADS * HEAD_DIM
RWKV_WIDTH = N_RWKV_HEADS * HEAD_DIM
ROPE_HALF = HEAD_DIM // 8
ROPE_THETA = 500000.0
DILATED_PATTERNS = ((128, 1), (512, 4), (2048, 16))
BAND = 128
DECAY_LORA = 64
AAA_LORA = 64
GATE_LORA = 128
ATT_COLS = 3 * ATT_WIDTH
SHIFT_COLS = 3 * RWKV_WIDTH + DECAY_LORA + AAA_LORA + GATE_LORA
LN_EPS = 1e-5
GN_EPS = 64e-5
L2_EPS = 1e-6
DECAY_SCALE = math.exp(-0.5)
NEG_BIG = -1e30

LANES = 128
CHUNK = 64
VMEM_LIMIT = 56 * 1024 * 1024

INV_FREQ = tuple(float(ROPE_THETA ** (-(j / ROPE_HALF))) for j in range(ROPE_HALF))


def _const_spec(shape):
    nd = len(shape)
    return pl.BlockSpec(shape, lambda *_: (0,) * nd, pipeline_mode=pl.Buffered(1))


def _inproj_kernel(x_ref, w_ref, pos_ref, *refs, tile):
    out_refs, rw_ref, stage = refs[:9], refs[9], refs[10]
    xb = x_ref[0].astype(BF16)
    n_blk = ATT_WIDTH // LANES

    dim = lax.broadcasted_iota(jnp.int32, (1, LANES), 1) % HEAD_DIM
    freq = jnp.zeros((1, LANES), F32)
    for j in range(ROPE_HALF):
        freq = jnp.where((dim == j) | (dim == j + ROPE_HALF), INV_FREQ[j], freq)
    ang = pos_ref[0].astype(F32) * freq
    c = jnp.cos(ang)
    sn = jnp.sin(ang)
    sa = jnp.where(dim < ROPE_HALF, sn, 0.0)
    sb = jnp.where(dim >= ROPE_HALF, sn, 0.0)

    def proj(lo, hi):
        return jnp.dot(xb, w_ref[:, lo:hi], preferred_element_type=F32)

    def rope(t, scale):
        outs = []
        for j in range(n_blk):
            blk = t[:, j * LANES:(j + 1) * LANES]
            up = pltpu.roll(blk, LANES - ROPE_HALF, 1)
            dn = pltpu.roll(blk, ROPE_HALF, 1)
            outs.append((blk * c - up * sa + dn * sb) * scale)
        return outs

    def emit(blocks, group):
        for j, blk in enumerate(blocks):
            stage[j] = blk
            out_refs[group][0, :, j * LANES:(j + 1) * LANES] = blk.astype(BF16)
        for pi, (_, d) in enumerate(DILATED_PATTERNS[1:], start=1):
            rows = tile // d
            for r in range(d):
                for j in range(n_blk):
                    cls = stage[j, pl.ds(r, rows, stride=d), :]
                    out_refs[3 * pi + group][0, r, :, j * LANES:(j + 1) * LANES] = cls.astype(BF16)

    def lane_blocks(t):
        return [t[:, j * LANES:(j + 1) * LANES] for j in range(n_blk)]

    emit(rope(proj(0, ATT_WIDTH), HEAD_DIM ** -0.5), 0)
    emit(rope(proj(ATT_WIDTH, 2 * ATT_WIDTH), 1.0), 1)
    emit(lane_blocks(proj(2 * ATT_WIDTH, ATT_COLS)), 2)
    rw_ref[0] = proj(ATT_COLS, ATT_COLS + SHIFT_COLS)


def _inproj(x, w_in_bf16, positions, tile):
    b, s, d_model = x.shape
    in_cols = w_in_bf16.shape[1]
    row = lambda bi, i: (bi, i, 0)
    cls = lambda bi, i: (bi, 0, i, 0)
    qkv_specs, qkv_shapes = [], []
    for _, d in DILATED_PATTERNS:
        for _ in range(3):
            if d == 1:
                qkv_specs.append(pl.BlockSpec((1, tile, ATT_WIDTH), row))
                qkv_shapes.append(jax.ShapeDtypeStruct((b, s, ATT_WIDTH), BF16))
            else:
                qkv_specs.append(pl.BlockSpec((1, d, tile // d, ATT_WIDTH), cls))
                qkv_shapes.append(jax.ShapeDtypeStruct((b, d, s // d, ATT_WIDTH), BF16))
    outs = pl.pallas_call(
        functools.partial(_inproj_kernel, tile=tile),
        grid=(b, s // tile),
        in_specs=[
            pl.BlockSpec((1, tile, d_model), row),
            _const_spec((d_model, in_cols)),
            pl.BlockSpec((1, tile, 1), row),
        ],
        out_specs=qkv_specs + [pl.BlockSpec((1, tile, SHIFT_COLS), row)],
        out_shape=qkv_shapes + [jax.ShapeDtypeStruct((b, s, SHIFT_COLS), F32)],
        scratch_shapes=[pltpu.VMEM((ATT_WIDTH // LANES, tile, LANES), F32)],
        compiler_params=pltpu.CompilerParams(
            dimension_semantics=("arbitrary", "arbitrary"), vmem_limit_bytes=VMEM_LIMIT),
        name="inproj",
    )(x, w_in_bf16, positions.reshape(b, s, 1))
    return outs[:9], outs[9]


SUPER = 2048
ATT_GROUP = 4
KV_ROWS = max(d * (BAND + SUPER // d) for _, d in DILATED_PATTERNS)


def _attn_kernel(*refs):
    ins, o_ref = refs[:15], refs[15]
    k_lo, k_hi, v_lo, v_hi, o_acc, lse_acc = refs[16:]
    sb = pl.program_id(1)

    qi = lax.broadcasted_iota(jnp.int32, (BAND, 2 * BAND), 0)
    kj = lax.broadcasted_iota(jnp.int32, (BAND, 2 * BAND), 1)
    band_bias = jnp.where((kj >= qi) & (kj <= qi + BAND), 0.0, NEG_BIG).astype(F32)
    first_bias = jnp.where((kj >= BAND) | (sb > 0), band_bias, NEG_BIG)
    lane_lo = lax.broadcasted_iota(jnp.int32, (BAND, LANES), 1) < HEAD_DIM

    def stage(dst_lo, dst_hi, rows, val):
        lane = lax.broadcasted_iota(jnp.int32, val.shape, 1)
        zero = jnp.zeros_like(val)
        dst_lo[rows, :] = jnp.where(lane < HEAD_DIM, val, zero)
        dst_hi[rows, :] = jnp.where(lane >= HEAD_DIM, val, zero)

    for pi, (_, d) in enumerate(DILATED_PATTERNS):
        q_ref, kc_ref, kp_ref, vc_ref, vp_ref = ins[5 * pi:5 * pi + 5]
        n = SUPER // d
        span = BAND + n
        for r in range(d):
            idx = (0,) if d == 1 else (0, r)
            base = r * span
            stage(k_lo, k_hi, slice(base, base + BAND), kp_ref[idx])
            stage(k_lo, k_hi, slice(base + BAND, base + span), kc_ref[idx])
            stage(v_lo, v_hi, slice(base, base + BAND), vp_ref[idx])
            stage(v_lo, v_hi, slice(base + BAND, base + span), vc_ref[idx])

        units = [(r, j) for j in range(n // BAND) for r in range(d)]
        for g0 in range(0, len(units), ATT_GROUP):
            group = units[g0:g0 + ATT_GROUP]
            scores = []
            for r, j in group:
                idx = (0,) if d == 1 else (0, r)
                q = q_ref[idx + (slice(j * BAND, (j + 1) * BAND), slice(None))]
                krows = slice(r * span + j * BAND, r * span + (j + 2) * BAND)
                k_bd = jnp.concatenate([k_lo[krows, :], k_hi[krows, :]], axis=0)
                scores.append(lax.dot_general(q, k_bd, _NT_DIMS, preferred_element_type=F32))
            probs, stats = [], []
            for (r, j), s in zip(group, scores):
                bias = first_bias if j == 0 else band_bias
                s0 = s[:, :2 * BAND] + bias
                s1 = s[:, 2 * BAND:] + bias
                m0 = jnp.max(s0, axis=1, keepdims=True)
                m1 = jnp.max(s1, axis=1, keepdims=True)
                p0 = jnp.exp(s0 - m0)
                p1 = jnp.exp(s1 - m1)
                l0 = jnp.sum(p0, axis=1, keepdims=True)
                l1 = jnp.sum(p1, axis=1, keepdims=True)
                probs.append(jnp.concatenate([p0, p1], axis=1).astype(BF16))
                stats.append((m0, m1, l0, l1))
            for (r, j), p, (m0, m1, l0, l1) in zip(group, probs, stats):
                krows = slice(r * span + j * BAND, r * span + (j + 2) * BAND)
                v_bd = jnp.concatenate([v_lo[krows, :], v_hi[krows, :]], axis=0)
                o = jnp.dot(p, v_bd, preferred_element_type=F32)
                inv_l = jnp.where(lane_lo, 1.0 / l0, 1.0 / l1)
                lse = jnp.where(lane_lo, m0 + jnp.log(l0), m1 + jnp.log(l1))
                if d == 1:
                    rows = slice(j * BAND, (j + 1) * BAND)
                else:
                    rows = pl.ds(j * BAND * d + r, BAND, stride=d)
                o_acc[pi, rows, :] = o * inv_l
                lse_acc[pi, rows, :] = lse

    la, lb, lc = lse_acc[0], lse_acc[1], lse_acc[2]
    m = jnp.maximum(jnp.maximum(la, lb), lc)
    ea, eb, ec = jnp.exp(la - m), jnp.exp(lb - m), jnp.exp(lc - m)
    o_ref[0] = ((ea * o_acc[0] + eb * o_acc[1] + ec * o_acc[2]) / (ea + eb + ec)).astype(BF16)


_NT_DIMS = (((1,), (1,)), ((), ()))


def _dilated_attention(qkv):
    b, s, w = qkv[0].shape
    args, specs = [], []
    for pi, (_, d) in enumerate(DILATED_PATTERNS):
        q, k, v = qkv[3 * pi:3 * pi + 3]
        n = SUPER // d
        sub = n // BAND
        if d == 1:
            cur = pl.BlockSpec((1, n, LANES), lambda bi, sb, hp: (bi, sb, hp))
            prev = pl.BlockSpec((1, BAND, LANES),
                                lambda bi, sb, hp, sub=sub: (bi, jnp.maximum(sb * sub - 1, 0), hp))
        else:
            cur = pl.BlockSpec((1, d, n, LANES), lambda bi, sb, hp: (bi, 0, sb, hp))
            prev = pl.BlockSpec((1, d, BAND, LANES),
                                lambda bi, sb, hp, sub=sub: (bi, 0, jnp.maximum(sb * sub - 1, 0), hp))
        args += [q, k, k, v, v]
        specs += [cur, cur, prev, cur, prev]
    kv_scratch = lambda: pltpu.VMEM((KV_ROWS, LANES), BF16)
    acc_scratch = lambda: pltpu.VMEM((len(DILATED_PATTERNS), SUPER, LANES), F32)
    return pl.pallas_call(
        _attn_kernel,
        grid=(b, s // SUPER, w // LANES),
        in_specs=specs,
        out_specs=pl.BlockSpec((1, SUPER, LANES), lambda bi, sb, hp: (bi, sb, hp)),
        out_shape=jax.ShapeDtypeStruct((b, s, w), BF16),
        scratch_shapes=[kv_scratch(), kv_scratch(), kv_scratch(), kv_scratch(),
                        acc_scratch(), acc_scratch()],
        compiler_params=pltpu.CompilerParams(
            dimension_semantics=("arbitrary",) * 3, vmem_limit_bytes=VMEM_LIMIT),
        name="dilated_attn",
    )(*args)


PAIR = 2 * HEAD_DIM
N_PAIRS = RWKV_WIDTH // PAIR

_NN = (((1,), (0,)), ((), ()))
_NT = (((1,), (1,)), ((), ()))
_TN = (((0,), (0,)), ((), ()))

PIECES_INVERSE = 1
PIECES_CHUNK_OPS = 1
PIECES_SCAN = 1


def _split(x, pieces):
    parts = []
    rem = x
    for i in range(pieces):
        part = rem.astype(BF16)
        parts.append(part)
        if i + 1 < pieces:
            rem = rem - part.astype(F32)
    return parts


def _dot(a_parts, b_parts, dims):
    order = max(len(a_parts), len(b_parts))
    acc = None
    for i, a in enumerate(a_parts):
        for j, b in enumerate(b_parts):
            if i + j < order:
                term = lax.dot_general(a, b, dims, preferred_element_type=F32)
                acc = term if acc is None else acc + term
    return acc


def _bd2(x):
    lane = lax.broadcasted_iota(jnp.int32, x.shape, 1)
    zero = jnp.zeros_like(x)
    return jnp.concatenate([jnp.where(lane < HEAD_DIM, x, zero),
                            jnp.where(lane >= HEAD_DIM, x, zero)], axis=0)


def _pair_nn(a, b, pieces):
    return _dot(_split(a, pieces), [_bd2(p) for p in _split(b, pieces)], _NN)


def _pair_nt(a, b, pieces):
    return _dot(_split(a, pieces), [_bd2(p) for p in _split(b, pieces)], _NT)


def _pair_tn(a, b, pieces):
    full = _dot(_split(a, pieces), _split(b, pieces), _TN)
    lane = lax.broadcasted_iota(jnp.int32, (HEAD_DIM, PAIR), 1)
    return jnp.where(lane < HEAD_DIM, full[0:HEAD_DIM], full[HEAD_DIM:])


def _seg_sum(x, seg):
    return _dot(_split(x, 2), [seg], _NN)


def _rwkv_kernel(y_ref, mu_ref, w0_ref, wdec_ref, a0_ref, waaa_ref, wgate_ref, kk_ref, ka_ref,
                 rk_ref, gng_ref, gnb_ref, seg_ref, o_ref, state, carry, *, tile):
    t = pl.program_id(1)

    @pl.when(t == 0)
    def _():
        state[...] = jnp.zeros_like(state)
        carry[...] = jnp.zeros_like(carry)

    seg = seg_ref[...]
    y = y_ref[0]
    rows_i = lax.broadcasted_iota(jnp.int32, y.shape, 0)
    prev = jnp.where(rows_i == 0, carry[...], pltpu.roll(y, 1, 0))
    carry[...] = y[tile - 1:tile, :]
    ys = y + mu_ref[...] * (prev - y)

    w = RWKV_WIDTH
    r = ys[:, 0:w]
    kr = ys[:, w:2 * w]
    vr = ys[:, 2 * w:3 * w]
    dw = ys[:, 3 * w:3 * w + DECAY_LORA]
    da = ys[:, 3 * w + DECAY_LORA:3 * w + DECAY_LORA + AAA_LORA]
    dg = ys[:, 3 * w + DECAY_LORA + AAA_LORA:]

    w_logit = w0_ref[...] + jnp.dot(jnp.tanh(dw).astype(BF16), wdec_ref[...],
                                    preferred_element_type=F32)
    ld = -DECAY_SCALE * jax.nn.sigmoid(w_logit)
    a = jax.nn.sigmoid(a0_ref[...] + jnp.dot(da.astype(BF16), waaa_ref[...],
                                             preferred_element_type=F32))
    g = jnp.dot(jax.nn.sigmoid(dg).astype(BF16), wgate_ref[...], preferred_element_type=F32)

    kk = kr * kk_ref[...]
    kk = kk / jnp.maximum(jnp.sqrt(_seg_sum(kk * kk, seg)), L2_EPS)
    k_mod = kr * (1.0 + (a - 1.0) * ka_ref[...])
    a_vec = -kk
    b_vec = kk * a

    n_chunks = tile // CHUNK
    ti = lax.broadcasted_iota(jnp.int32, (tile, tile), 0)
    si = lax.broadcasted_iota(jnp.int32, (tile, tile), 1)
    tri = ((ti // CHUNK == si // CHUNK) & (si <= ti)).astype(BF16)
    cum = _dot([tri], _split(ld, 3), _NN)
    cum_end = jnp.concatenate(
        [jnp.broadcast_to(cum[(c + 1) * CHUNK - 1:(c + 1) * CHUNK, :], (CHUNK, w))
         for c in range(n_chunks)], axis=0)
    p_in = jnp.exp(cum)
    inv_p = jnp.exp(-cum)
    to_end = jnp.exp(cum_end - cum)
    at = a_vec * jnp.exp(cum - ld)
    rt = r * p_in
    bt = b_vec * inv_p
    kt = k_mod * inv_p
    b_end = b_vec * to_end
    k_end = k_mod * to_end

    row = lax.broadcasted_iota(jnp.int32, (CHUNK, PAIR), 0)
    col = lax.broadcasted_iota(jnp.int32, (CHUNK, PAIR), 1) % HEAD_DIM
    strict = col < row
    incl = col <= row
    eye = (col == row).astype(F32)
    level_masks = [((row // (2 * s)) == (col // (2 * s))) & ((row % (2 * s)) >= s) & ((col % (2 * s)) < s)
                   for s in (1, 2, 4, 8, 16, 32)]

    probs = [(c, hp) for c in range(n_chunks) for hp in range(N_PAIRS)]

    def blk(x, prob):
        c, hp = prob
        return x[c * CHUNK:(c + 1) * CHUNK, hp * PAIR:(hp + 1) * PAIR]

    def masked(mask, x):
        return jnp.where(mask, x, 0.0)

    def bd2b(x):
        return _bd2(x.astype(BF16))

    def dot1(a, b, dims=_NN):
        return lax.dot_general(a.astype(BF16), b, dims, preferred_element_type=F32)

    lhs = [jnp.concatenate([blk(at, pr), blk(rt, pr)], axis=0) for pr in probs]
    sc = [dot1(l, jnp.concatenate([bd2b(blk(bt, pr)), bd2b(blk(kt, pr))], axis=0), _NT)
          for l, pr in zip(lhs, probs)]
    n_ab = [masked(strict, m[0:CHUNK, 0:PAIR]) for m in sc]
    a_ak = [masked(strict, m[0:CHUNK, PAIR:]) for m in sc]
    a_rb = [masked(incl, m[CHUNK:, 0:PAIR]) for m in sc]
    a_rk = [masked(incl, m[CHUNK:, PAIR:]) for m in sc]

    tinv = [eye + masked(level_masks[0], n) for n in n_ab]
    for lvl in range(1, len(level_masks)):
        x = [_pair_nn(masked(level_masks[lvl], n), ti_, PIECES_INVERSE) for n, ti_ in zip(n_ab, tinv)]
        tinv = [ti_ + _pair_nn(ti_, x_, PIECES_INVERSE) for ti_, x_ in zip(tinv, x)]

    akv = [dot1(jnp.concatenate([ak, rk], axis=0), bd2b(blk(vr, pr)))
           for ak, rk, pr in zip(a_ak, a_rk, probs)]
    tax = [dot1(ti_, jnp.concatenate([bd2b(blk(at, pr)), bd2b(m[0:CHUNK])], axis=1))
           for ti_, m, pr in zip(tinv, akv, probs)]
    a_bar = [m[:, 0:PAIR] for m in tax]
    x1 = [m[:, PAIR:] for m in tax]
    rbx = [dot1(m, jnp.concatenate([bd2b(ab), bd2b(x_)], axis=1)) for m, ab, x_ in zip(a_rb, a_bar, x1)]
    r_bar = [blk(rt, pr) + m[:, 0:PAIR] for m, pr in zip(rbx, probs)]
    o1 = [m[:, PAIR:] + kv[CHUNK:] for m, kv in zip(rbx, akv)]
    g_mat = [eye * blk(p_in, pr)[CHUNK - 1:CHUNK, :] + _pair_tn(ab, blk(b_end, pr), PIECES_CHUNK_OPS)
             for ab, pr in zip(a_bar, probs)]
    z_mat = [_pair_tn(jnp.concatenate([x_, blk(vr, pr)], axis=0),
                      jnp.concatenate([blk(b_end, pr), blk(k_end, pr)], axis=0), PIECES_CHUNK_OPS)
             for x_, pr in zip(x1, probs)]

    s_cur = [state[hp] for hp in range(N_PAIRS)]
    o_rows = []
    for c in range(n_chunks):
        o_c = []
        for hp in range(N_PAIRS):
            i = c * N_PAIRS + hp
            o_c.append(_pair_nt(r_bar[i], s_cur[hp], PIECES_SCAN) + o1[i])
        s_cur = [_pair_nn(s_cur[hp], g_mat[c * N_PAIRS + hp], PIECES_SCAN) + z_mat[c * N_PAIRS + hp]
                 for hp in range(N_PAIRS)]
        o_rows.append(jnp.concatenate(o_c, axis=1))
    for hp in range(N_PAIRS):
        state[hp] = s_cur[hp]
    o = jnp.concatenate(o_rows, axis=0)

    inv_n = 1.0 / HEAD_DIM
    mean = _seg_sum(o, seg) * inv_n
    d = o - mean
    var = _seg_sum(d * d, seg) * inv_n
    on = d * lax.rsqrt(var + GN_EPS) * gng_ref[...] + gnb_ref[...]
    bonus = _seg_sum(r * k_mod * rk_ref[...], seg) * vr
    o_ref[0] = ((on + bonus) * g).astype(BF16)


def _rwkv(y_rw, p, tile):
    b, s, cols = y_rw.shape
    w = RWKV_WIDTH
    row = lambda t: t.reshape(1, -1).astype(F32)
    lanes = jnp.arange(w) // HEAD_DIM
    seg = (lanes[:, None] == lanes[None, :]).astype(BF16)
    vec = lambda n: _const_spec((1, n))
    return pl.pallas_call(
        functools.partial(_rwkv_kernel, tile=tile),
        grid=(b, s // tile),
        in_specs=[
            pl.BlockSpec((1, tile, cols), lambda bi, t: (bi, t, 0)),
            vec(cols), vec(w), _const_spec((DECAY_LORA, w)), vec(w), _const_spec((AAA_LORA, w)),
            _const_spec((GATE_LORA, w)), vec(w), vec(w), vec(w), vec(w), vec(w),
            _const_spec((w, w)),
        ],
        out_specs=pl.BlockSpec((1, tile, w), lambda bi, t: (bi, t, 0)),
        out_shape=jax.ShapeDtypeStruct((b, s, w), BF16),
        scratch_shapes=[
            pltpu.VMEM((N_PAIRS, HEAD_DIM, PAIR), F32),
            pltpu.VMEM((1, cols), F32),
        ],
        compiler_params=pltpu.CompilerParams(
            dimension_semantics=("arbitrary", "arbitrary"), vmem_limit_bytes=VMEM_LIMIT),
        name="rwkv7",
    )(y_rw, row(p["mu_shift"]), row(p["w0"]), p["w_decay_up"].astype(BF16), row(p["a0"]),
      p["w_aaa_up"].astype(BF16), p["w_gate_up"].astype(BF16), row(p["k_k"]), row(p["k_a"]),
      row(p["r_k"]), row(p["gn_g"]), row(p["gn_b"]), seg)


def _layer_norm(x, g, b):
    mu = jnp.mean(x, axis=-1, keepdims=True)
    d = x - mu
    var = jnp.mean(d * d, axis=-1, keepdims=True)
    return d * lax.rsqrt(var + LN_EPS) * g + b


def _tail_kernel(oatt_ref, orw_ref, x_ref, wout_ref, g1_ref, b1_ref,
                 wg_ref, wu_ref, wd_ref, g2_ref, b2_ref, out_ref, *, alpha):
    mix = (jnp.dot(oatt_ref[...], wout_ref[0:ATT_WIDTH, :], preferred_element_type=F32)
           + jnp.dot(orw_ref[...], wout_ref[ATT_WIDTH:, :], preferred_element_type=F32))
    h = _layer_norm(alpha * x_ref[...] + mix, g1_ref[...], b1_ref[...])
    hb = h.astype(BF16)
    gate = jnp.dot(hb, wg_ref[...], preferred_element_type=F32)
    up = jnp.dot(hb, wu_ref[...], preferred_element_type=F32)
    act = (gate * jax.nn.sigmoid(gate) * up).astype(BF16)
    ffn = jnp.dot(act, wd_ref[...], preferred_element_type=F32)
    out_ref[...] = _layer_norm(alpha * h + ffn, g2_ref[...], b2_ref[...])


def _tail(o_att, o_rw, x2d, p, alpha, tile):
    n, d_model = x2d.shape
    hidden = p["w_ffn_gate"].shape[1]
    row = lambda i: (i, 0)
    vec = lambda t: t.reshape(1, -1).astype(F32)
    return pl.pallas_call(
        functools.partial(_tail_kernel, alpha=alpha),
        grid=(n // tile,),
        in_specs=[
            pl.BlockSpec((tile, ATT_WIDTH), row),
            pl.BlockSpec((tile, RWKV_WIDTH), row),
            pl.BlockSpec((tile, d_model), row),
            _const_spec((ATT_WIDTH + RWKV_WIDTH, d_model)),
            _const_spec((1, d_model)), _const_spec((1, d_model)),
            _const_spec((d_model, hidden)), _const_spec((d_model, hidden)),
            _const_spec((hidden, d_model)),
            _const_spec((1, d_model)), _const_spec((1, d_model)),
        ],
        out_specs=pl.BlockSpec((tile, d_model), row),
        out_shape=jax.ShapeDtypeStruct((n, d_model), F32),
        compiler_params=pltpu.CompilerParams(
            dimension_semantics=("arbitrary",), vmem_limit_bytes=VMEM_LIMIT),
        name="merge_outproj_ffn",
    )(o_att, o_rw, x2d, p["w_out"].astype(BF16), vec(p["ln_mix_g"]), vec(p["ln_mix_b"]),
      p["w_ffn_gate"].astype(BF16), p["w_ffn_up"].astype(BF16), p["w_ffn_down"].astype(BF16),
      vec(p["ln_ffn_g"]), vec(p["ln_ffn_b"]))


def kernel(x, positions, w_in, w_out, mu_shift, w0, w_decay_up, a0, w_aaa_up, w_gate_up, k_k, k_a,
           r_k, gn_g, gn_b, ln_mix_g, ln_mix_b, w_ffn_gate, w_ffn_up, w_ffn_down, ln_ffn_g,
           ln_ffn_b):
    b, s, d_model = x.shape
    depth = w_in.shape[0]
    alpha = (2.0 * depth) ** 0.25
    assert s % 2048 == 0 and d_model == 2 * ATT_WIDTH
    params = dict(w_in=w_in, w_out=w_out, mu_shift=mu_shift, w0=w0, w_decay_up=w_decay_up, a0=a0,
                  w_aaa_up=w_aaa_up, w_gate_up=w_gate_up, k_k=k_k, k_a=k_a, r_k=r_k, gn_g=gn_g,
                  gn_b=gn_b, ln_mix_g=ln_mix_g, ln_mix_b=ln_mix_b, w_ffn_gate=w_ffn_gate,
                  w_ffn_up=w_ffn_up, w_ffn_down=w_ffn_down, ln_ffn_g=ln_ffn_g, ln_ffn_b=ln_ffn_b)
    h2d = x.reshape(b * s, d_model)
    for layer in range(depth):
        p = {name: t[layer] for name, t in params.items()}
        qkv, y_rw = _inproj(h2d.reshape(b, s, d_model), p["w_in"].astype(BF16), positions, tile=512)
        o_att = _dilated_attention(qkv).reshape(b * s, ATT_WIDTH)
        o_rw = _rwkv(y_rw, p, tile=256).reshape(b * s, RWKV_WIDTH)
        h2d = _tail(o_att, o_rw, h2d, p, alpha, tile=512)
    return h2d.reshape(b, s, d_model)
```

```python
import functools
import math

import jax
import jax.numpy as jnp
from jax import lax
from jax.experimental import pallas as pl
from jax.experimental.pallas import tpu as pltpu

F32 = jnp.float32
BF16 = jnp.bfloat16

HEAD_DIM = 64
N_ATT_HEADS = 8
N_RWKV_HEADS = 8
ATT_WIDTH = N_ATT_HEADS * HEAD_DIM
RWKV_WIDTH = N_RWKV_HEADS * HEAD_DIM
ROPE_HALF = HEAD_DIM // 8
ROPE_THETA = 500000.0
DILATED_PATTERNS = ((128, 1), (512, 4), (2048, 16))
BAND = 128
DECAY_LORA = 64
AAA_LORA = 64
GATE_LORA = 128
ATT_COLS = 3 * ATT_WIDTH
SHIFT_COLS = 3 * RWKV_WIDTH + DECAY_LORA + AAA_LORA + GATE_LORA
LN_EPS = 1e-5
GN_EPS = 64e-5
L2_EPS = 1e-6
DECAY_SCALE = math.exp(-0.5)
NEG_BIG = -1e30

LANES = 128
CHUNK = 64
VMEM_LIMIT = 56 * 1024 * 1024

INV_FREQ = tuple(float(ROPE_THETA ** (-(j / ROPE_HALF))) for j in range(ROPE_HALF))


def _const_spec(shape):
    nd = len(shape)
    return pl.BlockSpec(shape, lambda *_: (0,) * nd, pipeline_mode=pl.Buffered(1))


def _inproj_kernel(x_ref, w_ref, pos_ref, *refs, tile):
    out_refs, rw_ref, stage = refs[:9], refs[9], refs[10]
    xb = x_ref[0].astype(BF16)
    n_blk = ATT_WIDTH // LANES

    dim = lax.broadcasted_iota(jnp.int32, (1, LANES), 1) % HEAD_DIM
    freq = jnp.zeros((1, LANES), F32)
    for j in range(ROPE_HALF):
        freq = jnp.where((dim == j) | (dim == j + ROPE_HALF), INV_FREQ[j], freq)
    ang = pos_ref[0].astype(F32) * freq
    c = jnp.cos(ang)
    sn = jnp.sin(ang)
    sa = jnp.where(dim < ROPE_HALF, sn, 0.0)
    sb = jnp.where(dim >= ROPE_HALF, sn, 0.0)

    def proj(lo, hi):
        return jnp.dot(xb, w_ref[:, lo:hi], preferred_element_type=F32)

    def rope(t, scale):
        outs = []
        for j in range(n_blk):
            blk = t[:, j * LANES:(j + 1) * LANES]
            up = pltpu.roll(blk, LANES - ROPE_HALF, 1)
            dn = pltpu.roll(blk, ROPE_HALF, 1)
            outs.append((blk * c - up * sa + dn * sb) * scale)
        return outs

    def emit(blocks, group):
        for j, blk in enumerate(blocks):
            stage[j] = blk
            out_refs[group][0, :, j * LANES:(j + 1) * LANES] = blk.astype(BF16)
        for pi, (_, d) in enumerate(DILATED_PATTERNS[1:], start=1):
            rows = tile // d
            for r in range(d):
                for j in range(n_blk):
                    cls = stage[j, pl.ds(r, rows, stride=d), :]
                    out_refs[3 * pi + group][0, r, :, j * LANES:(j + 1) * LANES] = cls.astype(BF16)

    def lane_blocks(t):
        return [t[:, j * LANES:(j + 1) * LANES] for j in range(n_blk)]

    emit(rope(proj(0, ATT_WIDTH), HEAD_DIM ** -0.5), 0)
    emit(rope(proj(ATT_WIDTH, 2 * ATT_WIDTH), 1.0), 1)
    emit(lane_blocks(proj(2 * ATT_WIDTH, ATT_COLS)), 2)
    rw_ref[0] = proj(ATT_COLS, ATT_COLS + SHIFT_COLS)


def _inproj(x, w_in_bf16, positions, tile):
    b, s, d_model = x.shape
    in_cols = w_in_bf16.shape[1]
    row = lambda bi, i: (bi, i, 0)
    cls = lambda bi, i: (bi, 0, i, 0)
    qkv_specs, qkv_shapes = [], []
    for _, d in DILATED_PATTERNS:
        for _ in range(3):
            if d == 1:
                qkv_specs.append(pl.BlockSpec((1, tile, ATT_WIDTH), row))
                qkv_shapes.append(jax.ShapeDtypeStruct((b, s, ATT_WIDTH), BF16))
            else:
                qkv_specs.append(pl.BlockSpec((1, d, tile // d, ATT_WIDTH), cls))
                qkv_shapes.append(jax.ShapeDtypeStruct((b, d, s // d, ATT_WIDTH), BF16))
    outs = pl.pallas_call(
        functools.partial(_inproj_kernel, tile=tile),
        grid=(b, s // tile),
        in_specs=[
            pl.BlockSpec((1, tile, d_model), row),
            _const_spec((d_model, in_cols)),
            pl.BlockSpec((1, tile, 1), row),
        ],
        out_specs=qkv_specs + [pl.BlockSpec((1, tile, SHIFT_COLS), row)],
        out_shape=qkv_shapes + [jax.ShapeDtypeStruct((b, s, SHIFT_COLS), F32)],
        scratch_shapes=[pltpu.VMEM((ATT_WIDTH // LANES, tile, LANES), F32)],
        compiler_params=pltpu.CompilerParams(
            dimension_semantics=("arbitrary", "arbitrary"), vmem_limit_bytes=VMEM_LIMIT),
        name="inproj",
    )(x, w_in_bf16, positions.reshape(b, s, 1))
    return outs[:9], outs[9]


SUPER = 2048
ATT_GROUP = 4


def _attn_kernel(*refs):
    ins, o_ref = refs[:15], refs[15]
    o_acc, lse_acc = refs[16:]
    sb = pl.program_id(1)

    qi = lax.broadcasted_iota(jnp.int32, (BAND, 2 * BAND), 0)
    kj = lax.broadcasted_iota(jnp.int32, (BAND, 2 * BAND), 1)
    band_bias = jnp.where((kj >= qi) & (kj <= qi + BAND), 0.0, NEG_BIG).astype(F32)
    first_bias = jnp.where((kj >= BAND) | (sb > 0), band_bias, NEG_BIG)
    lane_lo = lax.broadcasted_iota(jnp.int32, (BAND, LANES), 1) < HEAD_DIM

    zero_q = jnp.zeros((BAND, LANES), BF16)

    for pi, (_, d) in enumerate(DILATED_PATTERNS):
        q_ref, kc_ref, kp_ref, vc_ref, vp_ref = ins[5 * pi:5 * pi + 5]
        n = SUPER // d

        def window(cur_ref, prev_ref, r, j):
            idx = (0,) if d == 1 else (0, r)
            if j == 0:
                return jnp.concatenate(
                    [prev_ref[idx], cur_ref[idx + (slice(0, BAND), slice(None))]], axis=0)
            return cur_ref[idx + (slice((j - 1) * BAND, (j + 1) * BAND), slice(None))]

        units = [(r, j) for j in range(n // BAND) for r in range(d)]
        for g0 in range(0, len(units), ATT_GROUP):
            group = units[g0:g0 + ATT_GROUP]
            scores = []
            for r, j in group:
                idx = (0,) if d == 1 else (0, r)
                q = q_ref[idx + (slice(j * BAND, (j + 1) * BAND), slice(None))]
                k_win = window(kc_ref, kp_ref, r, j)
                scores.append((
                    lax.dot_general(jnp.where(lane_lo, q, zero_q), k_win, _NT_DIMS,
                                    preferred_element_type=F32),
                    lax.dot_general(jnp.where(lane_lo, zero_q, q), k_win, _NT_DIMS,
                                    preferred_element_type=F32)))
            probs, stats = [], []
            for (r, j), (s0, s1) in zip(group, scores):
                bias = first_bias if j == 0 else band_bias
                s0 = s0 + bias
                s1 = s1 + bias
                m0 = jnp.max(s0, axis=1, keepdims=True)
                m1 = jnp.max(s1, axis=1, keepdims=True)
                p0 = jnp.exp(s0 - m0)
                p1 = jnp.exp(s1 - m1)
                l0 = jnp.sum(p0, axis=1, keepdims=True)
                l1 = jnp.sum(p1, axis=1, keepdims=True)
                probs.append((p0.astype(BF16), p1.astype(BF16)))
                stats.append((m0, m1, l0, l1))
            for (r, j), (p0, p1), (m0, m1, l0, l1) in zip(group, probs, stats):
                v_win = window(vc_ref, vp_ref, r, j)
                o = jnp.where(lane_lo, jnp.dot(p0, v_win, preferred_element_type=F32),
                              jnp.dot(p1, v_win, preferred_element_type=F32))
                inv_l = jnp.where(lane_lo, 1.0 / l0, 1.0 / l1)
                lse = jnp.where(lane_lo, m0 + jnp.log(l0), m1 + jnp.log(l1))
                if d == 1:
                    rows = slice(j * BAND, (j + 1) * BAND)
                else:
                    rows = pl.ds(j * BAND * d + r, BAND, stride=d)
                o_acc[pi, rows, :] = o * inv_l
                lse_acc[pi, rows, :] = lse

    la, lb, lc = lse_acc[0], lse_acc[1], lse_acc[2]
    m = jnp.maximum(jnp.maximum(la, lb), lc)
    ea, eb, ec = jnp.exp(la - m), jnp.exp(lb - m), jnp.exp(lc - m)
    o_ref[0] = ((ea * o_acc[0] + eb * o_acc[1] + ec * o_acc[2]) / (ea + eb + ec)).astype(BF16)


_NT_DIMS = (((1,), (1,)), ((), ()))


def _dilated_attention(qkv):
    b, s, w = qkv[0].shape
    args, specs = [], []
    for pi, (_, d) in enumerate(DILATED_PATTERNS):
        q, k, v = qkv[3 * pi:3 * pi + 3]
        n = SUPER // d
        sub = n // BAND
        if d == 1:
            cur = pl.BlockSpec((1, n, LANES), lambda bi, sb, hp: (bi, sb, hp))
            prev = pl.BlockSpec((1, BAND, LANES),
                                lambda bi, sb, hp, sub=sub: (bi, jnp.maximum(sb * sub - 1, 0), hp))
        else:
            cur = pl.BlockSpec((1, d, n, LANES), lambda bi, sb, hp: (bi, 0, sb, hp))
            prev = pl.BlockSpec((1, d, BAND, LANES),
                                lambda bi, sb, hp, sub=sub: (bi, 0, jnp.maximum(sb * sub - 1, 0), hp))
        args += [q, k, k, v, v]
        specs += [cur, cur, prev, cur, prev]
    acc_scratch = lambda: pltpu.VMEM((len(DILATED_PATTERNS), SUPER, LANES), F32)
    return pl.pallas_call(
        _attn_kernel,
        grid=(b, s // SUPER, w // LANES),
        in_specs=specs,
        out_specs=pl.BlockSpec((1, SUPER, LANES), lambda bi, sb, hp: (bi, sb, hp)),
        out_shape=jax.ShapeDtypeStruct((b, s, w), BF16),
        scratch_shapes=[acc_scratch(), acc_scratch()],
        compiler_params=pltpu.CompilerParams(
            dimension_semantics=("arbitrary",) * 3, vmem_limit_bytes=VMEM_LIMIT),
        name="dilated_attn",
    )(*args)


PAIR = 2 * HEAD_DIM
N_PAIRS = RWKV_WIDTH // PAIR

_NN = (((1,), (0,)), ((), ()))
_NT = (((1,), (1,)), ((), ()))
_TN = (((0,), (0,)), ((), ()))

PIECES_INVERSE = 1
PIECES_CHUNK_OPS = 1
PIECES_SCAN = 1


def _split(x, pieces):
    parts = []
    rem = x
    for i in range(pieces):
        part = rem.astype(BF16)
        parts.append(part)
        if i + 1 < pieces:
            rem = rem - part.astype(F32)
    return parts


def _dot(a_parts, b_parts, dims):
    order = max(len(a_parts), len(b_parts))
    acc = None
    for i, a in enumerate(a_parts):
        for j, b in enumerate(b_parts):
            if i + j < order:
                term = lax.dot_general(a, b, dims, preferred_element_type=F32)
                acc = term if acc is None else acc + term
    return acc


def _bd2(x):
    lane = lax.broadcasted_iota(jnp.int32, x.shape, 1)
    zero = jnp.zeros_like(x)
    return jnp.concatenate([jnp.where(lane < HEAD_DIM, x, zero),
                            jnp.where(lane >= HEAD_DIM, x, zero)], axis=0)


def _pair_nn(a, b, pieces):
    return _dot(_split(a, pieces), [_bd2(p) for p in _split(b, pieces)], _NN)


def _pair_nt(a, b, pieces):
    return _dot(_split(a, pieces), [_bd2(p) for p in _split(b, pieces)], _NT)


def _pair_tn(a, b, pieces):
    full = _dot(_split(a, pieces), _split(b, pieces), _TN)
    lane = lax.broadcasted_iota(jnp.int32, (HEAD_DIM, PAIR), 1)
    return jnp.where(lane < HEAD_DIM, full[0:HEAD_DIM], full[HEAD_DIM:])


def _seg_sum(x, seg):
    return _dot(_split(x, 2), [seg], _NN)


def _rwkv_kernel(y_ref, mu_ref, w0_ref, wdec_ref, a0_ref, waaa_ref, wgate_ref, kk_ref, ka_ref,
                 rk_ref, gng_ref, gnb_ref, seg_ref, o_ref, state, carry, *, tile):
    t = pl.program_id(1)

    @pl.when(t == 0)
    def _():
        state[...] = jnp.zeros_like(state)
        carry[...] = jnp.zeros_like(carry)

    seg = seg_ref[...]
    y = y_ref[0]
    rows_i = lax.broadcasted_iota(jnp.int32, y.shape, 0)
    prev = jnp.where(rows_i == 0, carry[...], pltpu.roll(y, 1, 0))
    carry[...] = y[tile - 1:tile, :]
    ys = y + mu_ref[...] * (prev - y)

    w = RWKV_WIDTH
    r = ys[:, 0:w]
    kr = ys[:, w:2 * w]
    vr = ys[:, 2 * w:3 * w]
    dw = ys[:, 3 * w:3 * w + DECAY_LORA]
    da = ys[:, 3 * w + DECAY_LORA:3 * w + DECAY_LORA + AAA_LORA]
    dg = ys[:, 3 * w + DECAY_LORA + AAA_LORA:]

    w_logit = w0_ref[...] + jnp.dot(jnp.tanh(dw).astype(BF16), wdec_ref[...],
                                    preferred_element_type=F32)
    ld = -DECAY_SCALE * jax.nn.sigmoid(w_logit)
    a = jax.nn.sigmoid(a0_ref[...] + jnp.dot(da.astype(BF16), waaa_ref[...],
                                             preferred_element_type=F32))
    g = jnp.dot(jax.nn.sigmoid(dg).astype(BF16), wgate_ref[...], preferred_element_type=F32)

    kk = kr * kk_ref[...]
    kk = kk / jnp.maximum(jnp.sqrt(_seg_sum(kk * kk, seg)), L2_EPS)
    k_mod = kr * (1.0 + (a - 1.0) * ka_ref[...])
    a_vec = -kk
    b_vec = kk * a

    n_chunks = tile // CHUNK
    ti = lax.broadcasted_iota(jnp.int32, (tile, tile), 0)
    si = lax.broadcasted_iota(jnp.int32, (tile, tile), 1)
    tri = ((ti // CHUNK == si // CHUNK) & (si <= ti)).astype(BF16)
    cum = _dot([tri], _split(ld, 3), _NN)
    cum_end = jnp.concatenate(
        [jnp.broadcast_to(cum[(c + 1) * CHUNK - 1:(c + 1) * CHUNK, :], (CHUNK, w))
         for c in range(n_chunks)], axis=0)
    p_in = jnp.exp(cum)
    inv_p = jnp.exp(-cum)
    to_end = jnp.exp(cum_end - cum)
    at = a_vec * jnp.exp(cum - ld)
    rt = r * p_in
    bt = b_vec * inv_p
    kt = k_mod * inv_p
    b_end = b_vec * to_end
    k_end = k_mod * to_end

    row = lax.broadcasted_iota(jnp.int32, (CHUNK, PAIR), 0)
    col = lax.broadcasted_iota(jnp.int32, (CHUNK, PAIR), 1) % HEAD_DIM
    strict = col < row
    incl = col <= row
    eye = (col == row).astype(F32)
    level_masks = [((row // (2 * s)) == (col // (2 * s))) & ((row % (2 * s)) >= s) & ((col % (2 * s)) < s)
                   for s in (1, 2, 4, 8, 16, 32)]

    probs = [(c, hp) for c in range(n_chunks) for hp in range(N_PAIRS)]

    def blk(x, prob):
        c, hp = prob
        return x[c * CHUNK:(c + 1) * CHUNK, hp * PAIR:(hp + 1) * PAIR]

    def masked(mask, x):
        return jnp.where(mask, x, 0.0)

    def bd2b(x):
        return _bd2(x.astype(BF16))

    def dot1(a, b, dims=_NN):
        return lax.dot_general(a.astype(BF16), b, dims, preferred_element_type=F32)

    lhs = [jnp.concatenate([blk(at, pr), blk(rt, pr)], axis=0) for pr in probs]
    sc = [dot1(l, jnp.concatenate([bd2b(blk(bt, pr)), bd2b(blk(kt, pr))], axis=0), _NT)
          for l, pr in zip(lhs, probs)]
    n_ab = [masked(strict, m[0:CHUNK, 0:PAIR]) for m in sc]
    a_ak = [masked(strict, m[0:CHUNK, PAIR:]) for m in sc]
    a_rb = [masked(incl, m[CHUNK:, 0:PAIR]) for m in sc]
    a_rk = [masked(incl, m[CHUNK:, PAIR:]) for m in sc]

    tinv = [eye + masked(level_masks[0], n) for n in n_ab]
    for lvl in range(1, len(level_masks)):
        x = [_pair_nn(masked(level_masks[lvl], n), ti_, PIECES_INVERSE) for n, ti_ in zip(n_ab, tinv)]
        tinv = [ti_ + _pair_nn(ti_, x_, PIECES_INVERSE) for ti_, x_ in zip(tinv, x)]

    akv = [dot1(jnp.concatenate([ak, rk], axis=0), bd2b(blk(vr, pr)))
           for ak, rk, pr in zip(a_ak, a_rk, probs)]
    tax = [dot1(ti_, jnp.concatenate([bd2b(blk(at, pr)), bd2b(m[0:CHUNK])], axis=1))
           for ti_, m, pr in zip(tinv, akv, probs)]
    a_bar = [m[:, 0:PAIR] for m in tax]
    x1 = [m[:, PAIR:] for m in tax]
    rbx = [dot1(m, jnp.concatenate([bd2b(ab), bd2b(x_)], axis=1)) for m, ab, x_ in zip(a_rb, a_bar, x1)]
    r_bar = [blk(rt, pr) + m[:, 0:PAIR] for m, pr in zip(rbx, probs)]
    o1 = [m[:, PAIR:] + kv[CHUNK:] for m, kv in zip(rbx, akv)]
    g_mat = [eye * blk(p_in, pr)[CHUNK - 1:CHUNK, :] + _pair_tn(ab, blk(b_end, pr), PIECES_CHUNK_OPS)
             for ab, pr in zip(a_bar, probs)]
    z_mat = [_pair_tn(jnp.concatenate([x_, blk(vr, pr)], axis=0),
                      jnp.concatenate([blk(b_end, pr), blk(k_end, pr)], axis=0), PIECES_CHUNK_OPS)
             for x_, pr in zip(x1, probs)]

    s_cur = [state[hp] for hp in range(N_PAIRS)]
    o_rows = []
    for c in range(n_chunks):
        o_c = []
        for hp in range(N_PAIRS):
            i = c * N_PAIRS + hp
            o_c.append(_pair_nt(r_bar[i], s_cur[hp], PIECES_SCAN) + o1[i])
        s_cur = [_pair_nn(s_cur[hp], g_mat[c * N_PAIRS + hp], PIECES_SCAN) + z_mat[c * N_PAIRS + hp]
                 for hp in range(N_PAIRS)]
        o_rows.append(jnp.concatenate(o_c, axis=1))
    for hp in range(N_PAIRS):
        state[hp] = s_cur[hp]
    o = jnp.concatenate(o_rows, axis=0)

    inv_n = 1.0 / HEAD_DIM
    mean = _seg_sum(o, seg) * inv_n
    d = o - mean
    var = _seg_sum(d * d, seg) * inv_n
    on = d * lax.rsqrt(var + GN_EPS) * gng_ref[...] + gnb_ref[...]
    bonus = _seg_sum(r * k_mod * rk_ref[...], seg) * vr
    o_ref[0] = ((on + bonus) * g).astype(BF16)


def _rwkv(y_rw, p, tile):
    b, s, cols = y_rw.shape
    w = RWKV_WIDTH
    row = lambda t: t.reshape(1, -1).astype(F32)
    lanes = jnp.arange(w) // HEAD_DIM
    seg = (lanes[:, None] == lanes[None, :]).astype(BF16)
    vec = lambda n: _const_spec((1, n))
    return pl.pallas_call(
        functools.partial(_rwkv_kernel, tile=tile),
        grid=(b, s // tile),
        in_specs=[
            pl.BlockSpec((1, tile, cols), lambda bi, t: (bi, t, 0)),
            vec(cols), vec(w), _const_spec((DECAY_LORA, w)), vec(w), _const_spec((AAA_LORA, w)),
            _const_spec((GATE_LORA, w)), vec(w), vec(w), vec(w), vec(w), vec(w),
            _const_spec((w, w)),
        ],
        out_specs=pl.BlockSpec((1, tile, w), lambda bi, t: (bi, t, 0)),
        out_shape=jax.ShapeDtypeStruct((b, s, w), BF16),
        scratch_shapes=[
            pltpu.VMEM((N_PAIRS, HEAD_DIM, PAIR), F32),
            pltpu.VMEM((1, cols), F32),
        ],
        compiler_params=pltpu.CompilerParams(
            dimension_semantics=("arbitrary", "arbitrary"), vmem_limit_bytes=VMEM_LIMIT),
        name="rwkv7",
    )(y_rw, row(p["mu_shift"]), row(p["w0"]), p["w_decay_up"].astype(BF16), row(p["a0"]),
      p["w_aaa_up"].astype(BF16), p["w_gate_up"].astype(BF16), row(p["k_k"]), row(p["k_a"]),
      row(p["r_k"]), row(p["gn_g"]), row(p["gn_b"]), seg)


def _layer_norm(x, g, b):
    mu = jnp.mean(x, axis=-1, keepdims=True)
    d = x - mu
    var = jnp.mean(d * d, axis=-1, keepdims=True)
    return d * lax.rsqrt(var + LN_EPS) * g + b


def _tail_kernel(oatt_ref, orw_ref, x_ref, wout_ref, g1_ref, b1_ref,
                 wg_ref, wu_ref, wd_ref, g2_ref, b2_ref, out_ref, *, alpha):
    mix = (jnp.dot(oatt_ref[...], wout_ref[0:ATT_WIDTH, :], preferred_element_type=F32)
           + jnp.dot(orw_ref[...], wout_ref[ATT_WIDTH:, :], preferred_element_type=F32))
    h = _layer_norm(alpha * x_ref[...] + mix, g1_ref[...], b1_ref[...])
    hb = h.astype(BF16)
    gate = jnp.dot(hb, wg_ref[...], preferred_element_type=F32)
    up = jnp.dot(hb, wu_ref[...], preferred_element_type=F32)
    act = (gate * jax.nn.sigmoid(gate) * up).astype(BF16)
    ffn = jnp.dot(act, wd_ref[...], preferred_element_type=F32)
    out_ref[...] = _layer_norm(alpha * h + ffn, g2_ref[...], b2_ref[...])


def _tail(o_att, o_rw, x2d, p, alpha, tile):
    n, d_model = x2d.shape
    hidden = p["w_ffn_gate"].shape[1]
    row = lambda i: (i, 0)
    vec = lambda t: t.reshape(1, -1).astype(F32)
    return pl.pallas_call(
        functools.partial(_tail_kernel, alpha=alpha),
        grid=(n // tile,),
        in_specs=[
            pl.BlockSpec((tile, ATT_WIDTH), row),
            pl.BlockSpec((tile, RWKV_WIDTH), row),
            pl.BlockSpec((tile, d_model), row),
            _const_spec((ATT_WIDTH + RWKV_WIDTH, d_model)),
            _const_spec((1, d_model)), _const_spec((1, d_model)),
            _const_spec((d_model, hidden)), _const_spec((d_model, hidden)),
            _const_spec((hidden, d_model)),
            _const_spec((1, d_model)), _const_spec((1, d_model)),
        ],
        out_specs=pl.BlockSpec((tile, d_model), row),
        out_shape=jax.ShapeDtypeStruct((n, d_model), F32),
        compiler_params=pltpu.CompilerParams(
            dimension_semantics=("arbitrary",), vmem_limit_bytes=VMEM_LIMIT),
        name="merge_outproj_ffn",
    )(o_att, o_rw, x2d, p["w_out"].astype(BF16), vec(p["ln_mix_g"]), vec(p["ln_mix_b"]),
      p["w_ffn_gate"].astype(BF16), p["w_ffn_up"].astype(BF16), p["w_ffn_down"].astype(BF16),
      vec(p["ln_ffn_g"]), vec(p["ln_ffn_b"]))


def kernel(x, positions, w_in, w_out, mu_shift, w0, w_decay_up, a0, w_aaa_up, w_gate_up, k_k, k_a,
           r_k, gn_g, gn_b, ln_mix_g, ln_mix_b, w_ffn_gate, w_ffn_up, w_ffn_down, ln_ffn_g,
           ln_ffn_b):
    b, s, d_model = x.shape
    depth = w_in.shape[0]
    alpha = (2.0 * depth) ** 0.25
    assert s % 2048 == 0 and d_model == 2 * ATT_WIDTH
    params = dict(w_in=w_in, w_out=w_out, mu_shift=mu_shift, w0=w0, w_decay_up=w_decay_up, a0=a0,
                  w_aaa_up=w_aaa_up, w_gate_up=w_gate_up, k_k=k_k, k_a=k_a, r_k=r_k, gn_g=gn_g,
                  gn_b=gn_b, ln_mix_g=ln_mix_g, ln_mix_b=ln_mix_b, w_ffn_gate=w_ffn_gate,
                  w_ffn_up=w_ffn_up, w_ffn_down=w_ffn_down, ln_ffn_g=ln_ffn_g, ln_ffn_b=ln_ffn_b)
    h2d = x.reshape(b * s, d_model)
    for layer in range(depth):
        p = {name: t[layer] for name, t in params.items()}
        qkv, y_rw = _inproj(h2d.reshape(b, s, d_model), p["w_in"].astype(BF16), positions, tile=512)
        o_att = _dilated_attention(qkv).reshape(b * s, ATT_WIDTH)
        o_rw = _rwkv(y_rw, p, tile=256).reshape(b * s, RWKV_WIDTH)
        h2d = _tail(o_att, o_rw, h2d, p, alpha, tile=512)
    return h2d.reshape(b, s, d_model)
```

```python
import functools
import math

import jax
import jax.numpy as jnp
from jax import lax
from jax.experimental import pallas as pl
from jax.experimental.pallas import tpu as pltpu

F32 = jnp.float32
BF16 = jnp.bfloat16

HEAD_DIM = 64
N_ATT_HEADS = 8
N_RWKV_HEADS = 8
ATT_WIDTH = N_ATT_HEADS * HEAD_DIM
RWKV_WIDTH = N_RWKV_HEADS * HEAD_DIM
ROPE_HALF = HEAD_DIM // 8
ROPE_THETA = 500000.0
DILATED_PATTERNS = ((128, 1), (512, 4), (2048, 16))
BAND = 128
DECAY_LORA = 64
AAA_LORA = 64
GATE_LORA = 128
ATT_COLS = 3 * ATT_WIDTH
SHIFT_COLS = 3 * RWKV_WIDTH + DECAY_LORA + AAA_LORA + GATE_LORA
LN_EPS = 1e-5
GN_EPS = 64e-5
L2_EPS = 1e-6
DECAY_SCALE = math.exp(-0.5)
NEG_BIG = -1e30

LANES = 128
CHUNK = 64
VMEM_LIMIT = 56 * 1024 * 1024

INV_FREQ = tuple(float(ROPE_THETA ** (-(j / ROPE_HALF))) for j in range(ROPE_HALF))


def _const_spec(shape):
    nd = len(shape)
    return pl.BlockSpec(shape, lambda *_: (0,) * nd, pipeline_mode=pl.Buffered(1))


def _inproj_kernel(x_ref, w_ref, pos_ref, *refs, tile):
    out_refs, rw_ref, stages = refs[:9], refs[9], refs[10:12]
    stage = stages[0]
    xb = x_ref[0].astype(BF16)
    n_blk = ATT_WIDTH // LANES

    dim = lax.broadcasted_iota(jnp.int32, (1, LANES), 1) % HEAD_DIM
    freq = jnp.zeros((1, LANES), F32)
    for j in range(ROPE_HALF):
        freq = jnp.where((dim == j) | (dim == j + ROPE_HALF), INV_FREQ[j], freq)
    ang = pos_ref[0].astype(F32) * freq
    c = jnp.cos(ang)
    sn = jnp.sin(ang)
    sa = jnp.where(dim < ROPE_HALF, sn, 0.0)
    sb = jnp.where(dim >= ROPE_HALF, sn, 0.0)

    def proj(lo, hi):
        return jnp.dot(xb, w_ref[:, lo:hi], preferred_element_type=F32)

    def rope(t, scale):
        outs = []
        for j in range(n_blk):
            blk = t[:, j * LANES:(j + 1) * LANES]
            up = pltpu.roll(blk, LANES - ROPE_HALF, 1)
            dn = pltpu.roll(blk, ROPE_HALF, 1)
            outs.append((blk * c - up * sa + dn * sb) * scale)
        return outs

    def emit(blocks, group):
        for j, blk in enumerate(blocks):
            stage[j] = blk
            out_refs[group][0, :, j * LANES:(j + 1) * LANES] = blk.astype(BF16)
        prev_d, src = 1, stage
        for pi, (_, d) in enumerate(DILATED_PATTERNS[1:], start=1):
            ratio, rows, prev_rows = d // prev_d, tile // d, tile // prev_d
            dst = stages[pi % 2]
            keep = pi + 1 < len(DILATED_PATTERNS)
            for r_prev in range(prev_d):
                for k in range(ratio):
                    r = r_prev + prev_d * k
                    for j in range(n_blk):
                        cls = src[j, pl.ds(r_prev * prev_rows + k, rows, stride=ratio), :]
                        out_refs[3 * pi + group][0, r, :, j * LANES:(j + 1) * LANES] = cls.astype(BF16)
                        if keep:
                            dst[j, r * rows:(r + 1) * rows, :] = cls
            prev_d, src = d, dst

    def lane_blocks(t):
        return [t[:, j * LANES:(j + 1) * LANES] for j in range(n_blk)]

    emit(rope(proj(0, ATT_WIDTH), HEAD_DIM ** -0.5), 0)
    emit(rope(proj(ATT_WIDTH, 2 * ATT_WIDTH), 1.0), 1)
    emit(lane_blocks(proj(2 * ATT_WIDTH, ATT_COLS)), 2)
    rw_ref[0] = proj(ATT_COLS, ATT_COLS + SHIFT_COLS)


def _inproj(x, w_in_bf16, positions, tile):
    b, s, d_model = x.shape
    in_cols = w_in_bf16.shape[1]
    row = lambda bi, i: (bi, i, 0)
    cls = lambda bi, i: (bi, 0, i, 0)
    qkv_specs, qkv_shapes = [], []
    for _, d in DILATED_PATTERNS:
        for _ in range(3):
            if d == 1:
                qkv_specs.append(pl.BlockSpec((1, tile, ATT_WIDTH), row))
                qkv_shapes.append(jax.ShapeDtypeStruct((b, s, ATT_WIDTH), BF16))
            else:
                qkv_specs.append(pl.BlockSpec((1, d, tile // d, ATT_WIDTH), cls))
                qkv_shapes.append(jax.ShapeDtypeStruct((b, d, s // d, ATT_WIDTH), BF16))
    outs = pl.pallas_call(
        functools.partial(_inproj_kernel, tile=tile),
        grid=(b, s // tile),
        in_specs=[
            pl.BlockSpec((1, tile, d_model), row),
            _const_spec((d_model, in_cols)),
            pl.BlockSpec((1, tile, 1), row),
        ],
        out_specs=qkv_specs + [pl.BlockSpec((1, tile, SHIFT_COLS), row)],
        out_shape=qkv_shapes + [jax.ShapeDtypeStruct((b, s, SHIFT_COLS), F32)],
        scratch_shapes=[pltpu.VMEM((ATT_WIDTH // LANES, tile, LANES), F32)] * 2,
        compiler_params=pltpu.CompilerParams(
            dimension_semantics=("arbitrary", "arbitrary"), vmem_limit_bytes=VMEM_LIMIT),
        name="inproj",
    )(x, w_in_bf16, positions.reshape(b, s, 1))
    return outs[:9], outs[9]


SUPER = 2048
ATT_GROUP = 4


def _attn_kernel(*refs):
    ins, o_ref = refs[:15], refs[15]
    o_acc, lse_acc = refs[16:]
    sb = pl.program_id(1)

    qi = lax.broadcasted_iota(jnp.int32, (BAND, 2 * BAND), 0)
    kj = lax.broadcasted_iota(jnp.int32, (BAND, 2 * BAND), 1)
    band_bias = jnp.where((kj >= qi) & (kj <= qi + BAND), 0.0, NEG_BIG).astype(F32)
    first_bias = jnp.where((kj >= BAND) | (sb > 0), band_bias, NEG_BIG)
    lane_lo = lax.broadcasted_iota(jnp.int32, (BAND, LANES), 1) < HEAD_DIM

    zero_q = jnp.zeros((BAND, LANES), BF16)

    for pi, (_, d) in enumerate(DILATED_PATTERNS):
        q_ref, kc_ref, kp_ref, vc_ref, vp_ref = ins[5 * pi:5 * pi + 5]
        n = SUPER // d

        def window(cur_ref, prev_ref, r, j):
            idx = (0,) if d == 1 else (0, r)
            if j == 0:
                return jnp.concatenate(
                    [prev_ref[idx], cur_ref[idx + (slice(0, BAND), slice(None))]], axis=0)
            return cur_ref[idx + (slice((j - 1) * BAND, (j + 1) * BAND), slice(None))]

        units = [(r, j) for j in range(n // BAND) for r in range(d)]
        for g0 in range(0, len(units), ATT_GROUP):
            group = units[g0:g0 + ATT_GROUP]
            scores = []
            for r, j in group:
                idx = (0,) if d == 1 else (0, r)
                q = q_ref[idx + (slice(j * BAND, (j + 1) * BAND), slice(None))]
                k_win = window(kc_ref, kp_ref, r, j)
                scores.append((
                    lax.dot_general(jnp.where(lane_lo, q, zero_q), k_win, _NT_DIMS,
                                    preferred_element_type=F32),
                    lax.dot_general(jnp.where(lane_lo, zero_q, q), k_win, _NT_DIMS,
                                    preferred_element_type=F32)))
            probs, stats = [], []
            for (r, j), (s0, s1) in zip(group, scores):
                bias = first_bias if j == 0 else band_bias
                s0 = s0 + bias
                s1 = s1 + bias
                m0 = jnp.max(s0, axis=1, keepdims=True)
                m1 = jnp.max(s1, axis=1, keepdims=True)
                p0 = jnp.exp(s0 - m0)
                p1 = jnp.exp(s1 - m1)
                l0 = jnp.sum(p0, axis=1, keepdims=True)
                l1 = jnp.sum(p1, axis=1, keepdims=True)
                probs.append((p0.astype(BF16), p1.astype(BF16)))
                stats.append((m0, m1, l0, l1))
            for (r, j), (p0, p1), (m0, m1, l0, l1) in zip(group, probs, stats):
                v_win = window(vc_ref, vp_ref, r, j)
                o = jnp.where(lane_lo, jnp.dot(p0, v_win, preferred_element_type=F32),
                              jnp.dot(p1, v_win, preferred_element_type=F32))
                inv_l = jnp.where(lane_lo, 1.0 / l0, 1.0 / l1)
                lse = jnp.where(lane_lo, m0 + jnp.log(l0), m1 + jnp.log(l1))
                if d == 1:
                    rows = slice(j * BAND, (j + 1) * BAND)
                else:
                    rows = pl.ds(j * BAND * d + r, BAND, stride=d)
                o_acc[pi, rows, :] = o * inv_l
                lse_acc[pi, rows, :] = lse

    la, lb, lc = lse_acc[0], lse_acc[1], lse_acc[2]
    m = jnp.maximum(jnp.maximum(la, lb), lc)
    ea, eb, ec = jnp.exp(la - m), jnp.exp(lb - m), jnp.exp(lc - m)
    o_ref[0] = ((ea * o_acc[0] + eb * o_acc[1] + ec * o_acc[2]) / (ea + eb + ec)).astype(BF16)


_NT_DIMS = (((1,), (1,)), ((), ()))


def _dilated_attention(qkv):
    b, s, w = qkv[0].shape
    args, specs = [], []
    for pi, (_, d) in enumerate(DILATED_PATTERNS):
        q, k, v = qkv[3 * pi:3 * pi + 3]
        n = SUPER // d
        sub = n // BAND
        if d == 1:
            cur = pl.BlockSpec((1, n, LANES), lambda bi, sb, hp: (bi, sb, hp))
            prev = pl.BlockSpec((1, BAND, LANES),
                                lambda bi, sb, hp, sub=sub: (bi, jnp.maximum(sb * sub - 1, 0), hp))
        else:
            cur = pl.BlockSpec((1, d, n, LANES), lambda bi, sb, hp: (bi, 0, sb, hp))
            prev = pl.BlockSpec((1, d, BAND, LANES),
                                lambda bi, sb, hp, sub=sub: (bi, 0, jnp.maximum(sb * sub - 1, 0), hp))
        args += [q, k, k, v, v]
        specs += [cur, cur, prev, cur, prev]
    acc_scratch = lambda: pltpu.VMEM((len(DILATED_PATTERNS), SUPER, LANES), F32)
    return pl.pallas_call(
        _attn_kernel,
        grid=(b, s // SUPER, w // LANES),
        in_specs=specs,
        out_specs=pl.BlockSpec((1, SUPER, LANES), lambda bi, sb, hp: (bi, sb, hp)),
        out_shape=jax.ShapeDtypeStruct((b, s, w), BF16),
        scratch_shapes=[acc_scratch(), acc_scratch()],
        compiler_params=pltpu.CompilerParams(
            dimension_semantics=("arbitrary",) * 3, vmem_limit_bytes=VMEM_LIMIT),
        name="dilated_attn",
    )(*args)


PAIR = 2 * HEAD_DIM
N_PAIRS = RWKV_WIDTH // PAIR

_NN = (((1,), (0,)), ((), ()))
_NT = (((1,), (1,)), ((), ()))
_TN = (((0,), (0,)), ((), ()))

PIECES_INVERSE = 1
PIECES_CHUNK_OPS = 1
PIECES_SCAN = 1


def _split(x, pieces):
    parts = []
    rem = x
    for i in range(pieces):
        part = rem.astype(BF16)
        parts.append(part)
        if i + 1 < pieces:
            rem = rem - part.astype(F32)
    return parts


def _dot(a_parts, b_parts, dims):
    order = max(len(a_parts), len(b_parts))
    acc = None
    for i, a in enumerate(a_parts):
        for j, b in enumerate(b_parts):
            if i + j < order:
                term = lax.dot_general(a, b, dims, preferred_element_type=F32)
                acc = term if acc is None else acc + term
    return acc


def _bd2(x):
    lane = lax.broadcasted_iota(jnp.int32, x.shape, 1)
    zero = jnp.zeros_like(x)
    return jnp.concatenate([jnp.where(lane < HEAD_DIM, x, zero),
                            jnp.where(lane >= HEAD_DIM, x, zero)], axis=0)


def _pair_nn(a, b, pieces):
    return _dot(_split(a, pieces), [_bd2(p) for p in _split(b, pieces)], _NN)


def _pair_nt(a, b, pieces):
    return _dot(_split(a, pieces), [_bd2(p) for p in _split(b, pieces)], _NT)


def _pair_tn(a, b, pieces):
    full = _dot(_split(a, pieces), _split(b, pieces), _TN)
    lane = lax.broadcasted_iota(jnp.int32, (HEAD_DIM, PAIR), 1)
    return jnp.where(lane < HEAD_DIM, full[0:HEAD_DIM], full[HEAD_DIM:])


def _seg_sum(x, seg):
    return _dot(_split(x, 2), [seg], _NN)


def _rwkv_kernel(y_ref, mu_ref, w0_ref, wdec_ref, a0_ref, waaa_ref, wgate_ref, kk_ref, ka_ref,
                 rk_ref, gng_ref, gnb_ref, seg_ref, o_ref, state, carry, *, tile):
    t = pl.program_id(1)

    @pl.when(t == 0)
    def _():
        state[...] = jnp.zeros_like(state)
        carry[...] = jnp.zeros_like(carry)

    seg = seg_ref[...]
    y = y_ref[0]
    rows_i = lax.broadcasted_iota(jnp.int32, y.shape, 0)
    prev = jnp.where(rows_i == 0, carry[...], pltpu.roll(y, 1, 0))
    carry[...] = y[tile - 1:tile, :]
    ys = y + mu_ref[...] * (prev - y)

    w = RWKV_WIDTH
    r = ys[:, 0:w]
    kr = ys[:, w:2 * w]
    vr = ys[:, 2 * w:3 * w]
    dw = ys[:, 3 * w:3 * w + DECAY_LORA]
    da = ys[:, 3 * w + DECAY_LORA:3 * w + DECAY_LORA + AAA_LORA]
    dg = ys[:, 3 * w + DECAY_LORA + AAA_LORA:]

    w_logit = w0_ref[...] + jnp.dot(jnp.tanh(dw).astype(BF16), wdec_ref[...],
                                    preferred_element_type=F32)
    ld = -DECAY_SCALE * jax.nn.sigmoid(w_logit)
    a = jax.nn.sigmoid(a0_ref[...] + jnp.dot(da.astype(BF16), waaa_ref[...],
                                             preferred_element_type=F32))
    g = jnp.dot(jax.nn.sigmoid(dg).astype(BF16), wgate_ref[...], preferred_element_type=F32)

    kk = kr * kk_ref[...]
    kk = kk / jnp.maximum(jnp.sqrt(_seg_sum(kk * kk, seg)), L2_EPS)
    k_mod = kr * (1.0 + (a - 1.0) * ka_ref[...])
    a_vec = -kk
    b_vec = kk * a

    n_chunks = tile // CHUNK
    ti = lax.broadcasted_iota(jnp.int32, (tile, tile), 0)
    si = lax.broadcasted_iota(jnp.int32, (tile, tile), 1)
    tri = ((ti // CHUNK == si // CHUNK) & (si <= ti)).astype(BF16)
    cum = _dot([tri], _split(ld, 3), _NN)
    cum_end = jnp.concatenate(
        [jnp.broadcast_to(cum[(c + 1) * CHUNK - 1:(c + 1) * CHUNK, :], (CHUNK, w))
         for c in range(n_chunks)], axis=0)
    p_in = jnp.exp(cum)
    inv_p = jnp.exp(-cum)
    to_end = jnp.exp(cum_end - cum)
    at = a_vec * jnp.exp(cum - ld)
    rt = r * p_in
    bt = b_vec * inv_p
    kt = k_mod * inv_p
    b_end = b_vec * to_end
    k_end = k_mod * to_end

    row = lax.broadcasted_iota(jnp.int32, (CHUNK, PAIR), 0)
    col = lax.broadcasted_iota(jnp.int32, (CHUNK, PAIR), 1) % HEAD_DIM
    strict = col < row
    incl = col <= row
    eye = (col == row).astype(F32)
    level_masks = [((row // (2 * s)) == (col // (2 * s))) & ((row % (2 * s)) >= s) & ((col % (2 * s)) < s)
                   for s in (1, 2, 4, 8, 16, 32)]

    probs = [(c, hp) for c in range(n_chunks) for hp in range(N_PAIRS)]

    def blk(x, prob):
        c, hp = prob
        return x[c * CHUNK:(c + 1) * CHUNK, hp * PAIR:(hp + 1) * PAIR]

    def masked(mask, x):
        return jnp.where(mask, x, 0.0)

    def bd2b(x):
        return _bd2(x.astype(BF16))

    def dot1(a, b, dims=_NN):
        return lax.dot_general(a.astype(BF16), b, dims, preferred_element_type=F32)

    lhs = [jnp.concatenate([blk(at, pr), blk(rt, pr)], axis=0) for pr in probs]
    sc = [dot1(l, jnp.concatenate([bd2b(blk(bt, pr)), bd2b(blk(kt, pr))], axis=0), _NT)
          for l, pr in zip(lhs, probs)]
    n_ab = [masked(strict, m[0:CHUNK, 0:PAIR]) for m in sc]
    a_ak = [masked(strict, m[0:CHUNK, PAIR:]) for m in sc]
    a_rb = [masked(incl, m[CHUNK:, 0:PAIR]) for m in sc]
    a_rk = [masked(incl, m[CHUNK:, PAIR:]) for m in sc]

    tinv = [eye + masked(level_masks[0], n) for n in n_ab]
    for lvl in range(1, len(level_masks)):
        x = [_pair_nn(masked(level_masks[lvl], n), ti_, PIECES_INVERSE) for n, ti_ in zip(n_ab, tinv)]
        tinv = [ti_ + _pair_nn(ti_, x_, PIECES_INVERSE) for ti_, x_ in zip(tinv, x)]

    akv = [dot1(jnp.concatenate([ak, rk], axis=0), bd2b(blk(vr, pr)))
           for ak, rk, pr in zip(a_ak, a_rk, probs)]
    tax = [dot1(ti_, jnp.concatenate([bd2b(blk(at, pr)), bd2b(m[0:CHUNK])], axis=1))
           for ti_, m, pr in zip(tinv, akv, probs)]
    a_bar = [m[:, 0:PAIR] for m in tax]
    x1 = [m[:, PAIR:] for m in tax]
    rbx = [dot1(m, jnp.concatenate([bd2b(ab), bd2b(x_)], axis=1)) for m, ab, x_ in zip(a_rb, a_bar, x1)]
    r_bar = [blk(rt, pr) + m[:, 0:PAIR] for m, pr in zip(rbx, probs)]
    o1 = [m[:, PAIR:] + kv[CHUNK:] for m, kv in zip(rbx, akv)]
    g_mat = [eye * blk(p_in, pr)[CHUNK - 1:CHUNK, :] + _pair_tn(ab, blk(b_end, pr), PIECES_CHUNK_OPS)
             for ab, pr in zip(a_bar, probs)]
    z_mat = [_pair_tn(jnp.concatenate([x_, blk(vr, pr)], axis=0),
                      jnp.concatenate([blk(b_end, pr), blk(k_end, pr)], axis=0), PIECES_CHUNK_OPS)
             for x_, pr in zip(x1, probs)]

    s_cur = [state[hp] for hp in range(N_PAIRS)]
    o_rows = []
    for c in range(n_chunks):
        o_c = []
        for hp in range(N_PAIRS):
            i = c * N_PAIRS + hp
            o_c.append(_pair_nt(r_bar[i], s_cur[hp], PIECES_SCAN) + o1[i])
        s_cur = [_pair_nn(s_cur[hp], g_mat[c * N_PAIRS + hp], PIECES_SCAN) + z_mat[c * N_PAIRS + hp]
                 for hp in range(N_PAIRS)]
        o_rows.append(jnp.concatenate(o_c, axis=1))
    for hp in range(N_PAIRS):
        state[hp] = s_cur[hp]
    o = jnp.concatenate(o_rows, axis=0)

    inv_n = 1.0 / HEAD_DIM
    mean = _seg_sum(o, seg) * inv_n
    d = o - mean
    var = _seg_sum(d * d, seg) * inv_n
    on = d * lax.rsqrt(var + GN_EPS) * gng_ref[...] + gnb_ref[...]
    bonus = _seg_sum(r * k_mod * rk_ref[...], seg) * vr
    o_ref[0] = ((on + bonus) * g).astype(BF16)


def _rwkv(y_rw, p, tile):
    b, s, cols = y_rw.shape
    w = RWKV_WIDTH
    row = lambda t: t.reshape(1, -1).astype(F32)
    lanes = jnp.arange(w) // HEAD_DIM
    seg = (lanes[:, None] == lanes[None, :]).astype(BF16)
    vec = lambda n: _const_spec((1, n))
    return pl.pallas_call(
        functools.partial(_rwkv_kernel, tile=tile),
        grid=(b, s // tile),
        in_specs=[
            pl.BlockSpec((1, tile, cols), lambda bi, t: (bi, t, 0)),
            vec(cols), vec(w), _const_spec((DECAY_LORA, w)), vec(w), _const_spec((AAA_LORA, w)),
            _const_spec((GATE_LORA, w)), vec(w), vec(w), vec(w), vec(w), vec(w),
            _const_spec((w, w)),
        ],
        out_specs=pl.BlockSpec((1, tile, w), lambda bi, t: (bi, t, 0)),
        out_shape=jax.ShapeDtypeStruct((b, s, w), BF16),
        scratch_shapes=[
            pltpu.VMEM((N_PAIRS, HEAD_DIM, PAIR), F32),
            pltpu.VMEM((1, cols), F32),
        ],
        compiler_params=pltpu.CompilerParams(
            dimension_semantics=("arbitrary", "arbitrary"), vmem_limit_bytes=VMEM_LIMIT),
        name="rwkv7",
    )(y_rw, row(p["mu_shift"]), row(p["w0"]), p["w_decay_up"].astype(BF16), row(p["a0"]),
      p["w_aaa_up"].astype(BF16), p["w_gate_up"].astype(BF16), row(p["k_k"]), row(p["k_a"]),
      row(p["r_k"]), row(p["gn_g"]), row(p["gn_b"]), seg)


def _layer_norm(x, g, b):
    mu = jnp.mean(x, axis=-1, keepdims=True)
    d = x - mu
    var = jnp.mean(d * d, axis=-1, keepdims=True)
    return d * lax.rsqrt(var + LN_EPS) * g + b


def _tail_kernel(oatt_ref, orw_ref, x_ref, wout_ref, g1_ref, b1_ref,
                 wg_ref, wu_ref, wd_ref, g2_ref, b2_ref, out_ref, *, alpha):
    mix = (jnp.dot(oatt_ref[...], wout_ref[0:ATT_WIDTH, :], preferred_element_type=F32)
           + jnp.dot(orw_ref[...], wout_ref[ATT_WIDTH:, :], preferred_element_type=F32))
    h = _layer_norm(alpha * x_ref[...] + mix, g1_ref[...], b1_ref[...])
    hb = h.astype(BF16)
    gate = jnp.dot(hb, wg_ref[...], preferred_element_type=F32)
    up = jnp.dot(hb, wu_ref[...], preferred_element_type=F32)
    act = (gate * jax.nn.sigmoid(gate) * up).astype(BF16)
    ffn = jnp.dot(act, wd_ref[...], preferred_element_type=F32)
    out_ref[...] = _layer_norm(alpha * h + ffn, g2_ref[...], b2_ref[...])


def _tail(o_att, o_rw, x2d, p, alpha, tile):
    n, d_model = x2d.shape
    hidden = p["w_ffn_gate"].shape[1]
    row = lambda i: (i, 0)
    vec = lambda t: t.reshape(1, -1).astype(F32)
    return pl.pallas_call(
        functools.partial(_tail_kernel, alpha=alpha),
        grid=(n // tile,),
        in_specs=[
            pl.BlockSpec((tile, ATT_WIDTH), row),
            pl.BlockSpec((tile, RWKV_WIDTH), row),
            pl.BlockSpec((tile, d_model), row),
            _const_spec((ATT_WIDTH + RWKV_WIDTH, d_model)),
            _const_spec((1, d_model)), _const_spec((1, d_model)),
            _const_spec((d_model, hidden)), _const_spec((d_model, hidden)),
            _const_spec((hidden, d_model)),
            _const_spec((1, d_model)), _const_spec((1, d_model)),
        ],
        out_specs=pl.BlockSpec((tile, d_model), row),
        out_shape=jax.ShapeDtypeStruct((n, d_model), F32),
        compiler_params=pltpu.CompilerParams(
            dimension_semantics=("arbitrary",), vmem_limit_bytes=VMEM_LIMIT),
        name="merge_outproj_ffn",
    )(o_att, o_rw, x2d, p["w_out"].astype(BF16), vec(p["ln_mix_g"]), vec(p["ln_mix_b"]),
      p["w_ffn_gate"].astype(BF16), p["w_ffn_up"].astype(BF16), p["w_ffn_down"].astype(BF16),
      vec(p["ln_ffn_g"]), vec(p["ln_ffn_b"]))


def kernel(x, positions, w_in, w_out, mu_shift, w0, w_decay_up, a0, w_aaa_up, w_gate_up, k_k, k_a,
           r_k, gn_g, gn_b, ln_mix_g, ln_mix_b, w_ffn_gate, w_ffn_up, w_ffn_down, ln_ffn_g,
           ln_ffn_b):
    b, s, d_model = x.shape
    depth = w_in.shape[0]
    alpha = (2.0 * depth) ** 0.25
    assert s % 2048 == 0 and d_model == 2 * ATT_WIDTH
    params = dict(w_in=w_in, w_out=w_out, mu_shift=mu_shift, w0=w0, w_decay_up=w_decay_up, a0=a0,
                  w_aaa_up=w_aaa_up, w_gate_up=w_gate_up, k_k=k_k, k_a=k_a, r_k=r_k, gn_g=gn_g,
                  gn_b=gn_b, ln_mix_g=ln_mix_g, ln_mix_b=ln_mix_b, w_ffn_gate=w_ffn_gate,
                  w_ffn_up=w_ffn_up, w_ffn_down=w_ffn_down, ln_ffn_g=ln_ffn_g, ln_ffn_b=ln_ffn_b)
    h2d = x.reshape(b * s, d_model)
    for layer in range(depth):
        p = {name: t[layer] for name, t in params.items()}
        qkv, y_rw = _inproj(h2d.reshape(b, s, d_model), p["w_in"].astype(BF16), positions, tile=512)
        o_att = _dilated_attention(qkv).reshape(b * s, ATT_WIDTH)
        o_rw = _rwkv(y_rw, p, tile=256).reshape(b * s, RWKV_WIDTH)
        h2d = _tail(o_att, o_rw, h2d, p, alpha, tile=512)
    return h2d.reshape(b, s, d_model)
```

```python
import functools
import math

import jax
import jax.numpy as jnp
from jax import lax
from jax.experimental import pallas as pl
from jax.experimental.pallas import tpu as pltpu

F32 = jnp.float32
BF16 = jnp.bfloat16

HEAD_DIM = 64
N_ATT_HEADS = 8
N_RWKV_HEADS = 8
ATT_WIDTH = N_ATT_HEADS * HEAD_DIM
RWKV_WIDTH = N_RWKV_HEADS * HEAD_DIM
ROPE_HALF = HEAD_DIM // 8
ROPE_THETA = 500000.0
DILATED_PATTERNS = ((128, 1), (512, 4), (2048, 16))
BAND = 128
DECAY_LORA = 64
AAA_LORA = 64
GATE_LORA = 128
ATT_COLS = 3 * ATT_WIDTH
SHIFT_COLS = 3 * RWKV_WIDTH + DECAY_LORA + AAA_LORA + GATE_LORA
LN_EPS = 1e-5
GN_EPS = 64e-5
L2_EPS = 1e-6
DECAY_SCALE = math.exp(-0.5)
NEG_BIG = -1e30

LANES = 128
CHUNK = 64
VMEM_LIMIT = 56 * 1024 * 1024

INV_FREQ = tuple(float(ROPE_THETA ** (-(j / ROPE_HALF))) for j in range(ROPE_HALF))


def _const_spec(shape):
    nd = len(shape)
    return pl.BlockSpec(shape, lambda *_: (0,) * nd, pipeline_mode=pl.Buffered(1))


def _inproj_kernel(x_ref, w_ref, pos_ref, *refs, tile):
    out_refs, rw_ref, stages = refs[:9], refs[9], refs[10:12]
    stage = stages[0]
    xb = x_ref[0].astype(BF16)
    n_blk = ATT_WIDTH // LANES

    dim = lax.broadcasted_iota(jnp.int32, (1, LANES), 1) % HEAD_DIM
    freq = jnp.zeros((1, LANES), F32)
    for j in range(ROPE_HALF):
        freq = jnp.where((dim == j) | (dim == j + ROPE_HALF), INV_FREQ[j], freq)
    ang = pos_ref[0].astype(F32) * freq
    c = jnp.cos(ang)
    sn = jnp.sin(ang)
    sa = jnp.where(dim < ROPE_HALF, sn, 0.0)
    sb = jnp.where(dim >= ROPE_HALF, sn, 0.0)

    def proj(lo, hi):
        return jnp.dot(xb, w_ref[:, lo:hi], preferred_element_type=F32)

    def rope(t, scale):
        outs = []
        for j in range(n_blk):
            blk = t[:, j * LANES:(j + 1) * LANES]
            up = pltpu.roll(blk, LANES - ROPE_HALF, 1)
            dn = pltpu.roll(blk, ROPE_HALF, 1)
            outs.append((blk * c - up * sa + dn * sb) * scale)
        return outs

    def emit(blocks, group):
        for j, blk in enumerate(blocks):
            stage[j] = blk
            out_refs[group][0, :, j * LANES:(j + 1) * LANES] = blk.astype(BF16)
        prev_d, src = 1, stage
        for pi, (_, d) in enumerate(DILATED_PATTERNS[1:], start=1):
            ratio, rows, prev_rows = d // prev_d, tile // d, tile // prev_d
            dst = stages[pi % 2]
            keep = pi + 1 < len(DILATED_PATTERNS)
            for r_prev in range(prev_d):
                for k in range(ratio):
                    r = r_prev + prev_d * k
                    for j in range(n_blk):
                        cls = src[j, pl.ds(r_prev * prev_rows + k, rows, stride=ratio), :]
                        out_refs[3 * pi + group][0, r, :, j * LANES:(j + 1) * LANES] = cls.astype(BF16)
                        if keep:
                            dst[j, r * rows:(r + 1) * rows, :] = cls
            prev_d, src = d, dst

    def lane_blocks(t):
        return [t[:, j * LANES:(j + 1) * LANES] for j in range(n_blk)]

    emit(rope(proj(0, ATT_WIDTH), HEAD_DIM ** -0.5), 0)
    emit(rope(proj(ATT_WIDTH, 2 * ATT_WIDTH), 1.0), 1)
    emit(lane_blocks(proj(2 * ATT_WIDTH, ATT_COLS)), 2)
    rw_ref[0] = proj(ATT_COLS, ATT_COLS + SHIFT_COLS)


def _inproj(x, w_in_bf16, positions, tile):
    b, s, d_model = x.shape
    in_cols = w_in_bf16.shape[1]
    row = lambda bi, i: (bi, i, 0)
    cls = lambda bi, i: (bi, 0, i, 0)
    qkv_specs, qkv_shapes = [], []
    for _, d in DILATED_PATTERNS:
        for _ in range(3):
            if d == 1:
                qkv_specs.append(pl.BlockSpec((1, tile, ATT_WIDTH), row))
                qkv_shapes.append(jax.ShapeDtypeStruct((b, s, ATT_WIDTH), BF16))
            else:
                qkv_specs.append(pl.BlockSpec((1, d, tile // d, ATT_WIDTH), cls))
                qkv_shapes.append(jax.ShapeDtypeStruct((b, d, s // d, ATT_WIDTH), BF16))
    outs = pl.pallas_call(
        functools.partial(_inproj_kernel, tile=tile),
        grid=(b, s // tile),
        in_specs=[
            pl.BlockSpec((1, tile, d_model), row),
            _const_spec((d_model, in_cols)),
            pl.BlockSpec((1, tile, 1), row),
        ],
        out_specs=qkv_specs + [pl.BlockSpec((1, tile, SHIFT_COLS), row)],
        out_shape=qkv_shapes + [jax.ShapeDtypeStruct((b, s, SHIFT_COLS), F32)],
        scratch_shapes=[pltpu.VMEM((ATT_WIDTH // LANES, tile, LANES), F32)] * 2,
        compiler_params=pltpu.CompilerParams(
            dimension_semantics=("arbitrary", "arbitrary"), vmem_limit_bytes=VMEM_LIMIT),
        name="inproj",
    )(x, w_in_bf16, positions.reshape(b, s, 1))
    return outs[:9], outs[9]


SUPER = 2048
ATT_GROUP = 4


def _attn_kernel(*refs):
    ins, o_ref = refs[:15], refs[15]
    o_acc, lse_acc, mid = refs[16:]
    sb = pl.program_id(1)
    fine, coarse = DILATED_PATTERNS[1][1], DILATED_PATTERNS[2][1]

    qi = lax.broadcasted_iota(jnp.int32, (BAND, 2 * BAND), 0)
    kj = lax.broadcasted_iota(jnp.int32, (BAND, 2 * BAND), 1)
    band_bias = jnp.where((kj >= qi) & (kj <= qi + BAND), 0.0, NEG_BIG).astype(F32)
    first_bias = jnp.where((kj >= BAND) | (sb > 0), band_bias, NEG_BIG)
    lane_lo = lax.broadcasted_iota(jnp.int32, (BAND, LANES), 1) < HEAD_DIM

    zero_q = jnp.zeros((BAND, LANES), BF16)

    for pi, (_, d) in enumerate(DILATED_PATTERNS):
        q_ref, kc_ref, kp_ref, vc_ref, vp_ref = ins[5 * pi:5 * pi + 5]
        n = SUPER // d

        def window(cur_ref, prev_ref, r, j):
            idx = (0,) if d == 1 else (0, r)
            if j == 0:
                return jnp.concatenate(
                    [prev_ref[idx], cur_ref[idx + (slice(0, BAND), slice(None))]], axis=0)
            return cur_ref[idx + (slice((j - 1) * BAND, (j + 1) * BAND), slice(None))]

        units = [(r, j) for j in range(n // BAND) for r in range(d)]
        for g0 in range(0, len(units), ATT_GROUP):
            group = units[g0:g0 + ATT_GROUP]
            scores = []
            for r, j in group:
                idx = (0,) if d == 1 else (0, r)
                q = q_ref[idx + (slice(j * BAND, (j + 1) * BAND), slice(None))]
                k_win = window(kc_ref, kp_ref, r, j)
                scores.append((
                    lax.dot_general(jnp.where(lane_lo, q, zero_q), k_win, _NT_DIMS,
                                    preferred_element_type=F32),
                    lax.dot_general(jnp.where(lane_lo, zero_q, q), k_win, _NT_DIMS,
                                    preferred_element_type=F32)))
            probs, stats = [], []
            for (r, j), (s0, s1) in zip(group, scores):
                bias = first_bias if j == 0 else band_bias
                s0 = s0 + bias
                s1 = s1 + bias
                m0 = jnp.max(s0, axis=1, keepdims=True)
                m1 = jnp.max(s1, axis=1, keepdims=True)
                p0 = jnp.exp(s0 - m0)
                p1 = jnp.exp(s1 - m1)
                l0 = jnp.sum(p0, axis=1, keepdims=True)
                l1 = jnp.sum(p1, axis=1, keepdims=True)
                probs.append((p0.astype(BF16), p1.astype(BF16)))
                stats.append((m0, m1, l0, l1))
            for (r, j), (p0, p1), (m0, m1, l0, l1) in zip(group, probs, stats):
                v_win = window(vc_ref, vp_ref, r, j)
                o = jnp.where(lane_lo, jnp.dot(p0, v_win, preferred_element_type=F32),
                              jnp.dot(p1, v_win, preferred_element_type=F32))
                inv_l = jnp.where(lane_lo, 1.0 / l0, 1.0 / l1)
                lse = jnp.where(lane_lo, m0 + jnp.log(l0), m1 + jnp.log(l1))
                if d == 1:
                    rows = slice(j * BAND, (j + 1) * BAND)
                elif d == coarse:
                    rows = pl.ds((r % fine) * (SUPER // fine) + j * BAND * (d // fine) + r // fine,
                                 BAND, stride=d // fine)
                    mid[0, rows, :] = o * inv_l
                    mid[1, rows, :] = lse
                    continue
                else:
                    rows = pl.ds(j * BAND * d + r, BAND, stride=d)
                o_acc[pi, rows, :] = o * inv_l
                lse_acc[pi, rows, :] = lse

        if d == coarse:
            for rf in range(fine):
                src = slice(rf * (SUPER // fine), (rf + 1) * (SUPER // fine))
                dst = pl.ds(rf, SUPER // fine, stride=fine)
                o_acc[pi, dst, :] = mid[0, src, :]
                lse_acc[pi, dst, :] = mid[1, src, :]

    la, lb, lc = lse_acc[0], lse_acc[1], lse_acc[2]
    m = jnp.maximum(jnp.maximum(la, lb), lc)
    ea, eb, ec = jnp.exp(la - m), jnp.exp(lb - m), jnp.exp(lc - m)
    o_ref[0] = ((ea * o_acc[0] + eb * o_acc[1] + ec * o_acc[2]) / (ea + eb + ec)).astype(BF16)


_NT_DIMS = (((1,), (1,)), ((), ()))


def _dilated_attention(qkv):
    b, s, w = qkv[0].shape
    args, specs = [], []
    for pi, (_, d) in enumerate(DILATED_PATTERNS):
        q, k, v = qkv[3 * pi:3 * pi + 3]
        n = SUPER // d
        sub = n // BAND
        if d == 1:
            cur = pl.BlockSpec((1, n, LANES), lambda bi, sb, hp: (bi, sb, hp))
            prev = pl.BlockSpec((1, BAND, LANES),
                                lambda bi, sb, hp, sub=sub: (bi, jnp.maximum(sb * sub - 1, 0), hp))
        else:
            cur = pl.BlockSpec((1, d, n, LANES), lambda bi, sb, hp: (bi, 0, sb, hp))
            prev = pl.BlockSpec((1, d, BAND, LANES),
                                lambda bi, sb, hp, sub=sub: (bi, 0, jnp.maximum(sb * sub - 1, 0), hp))
        args += [q, k, k, v, v]
        specs += [cur, cur, prev, cur, prev]
    acc_scratch = lambda: pltpu.VMEM((len(DILATED_PATTERNS), SUPER, LANES), F32)
    return pl.pallas_call(
        _attn_kernel,
        grid=(b, s // SUPER, w // LANES),
        in_specs=specs,
        out_specs=pl.BlockSpec((1, SUPER, LANES), lambda bi, sb, hp: (bi, sb, hp)),
        out_shape=jax.ShapeDtypeStruct((b, s, w), BF16),
        scratch_shapes=[acc_scratch(), acc_scratch(), pltpu.VMEM((2, SUPER, LANES), F32)],
        compiler_params=pltpu.CompilerParams(
            dimension_semantics=("arbitrary",) * 3, vmem_limit_bytes=VMEM_LIMIT),
        name="dilated_attn",
    )(*args)


PAIR = 2 * HEAD_DIM
N_PAIRS = RWKV_WIDTH // PAIR

_NN = (((1,), (0,)), ((), ()))
_NT = (((1,), (1,)), ((), ()))
_TN = (((0,), (0,)), ((), ()))

PIECES_INVERSE = 1
PIECES_CHUNK_OPS = 1
PIECES_SCAN = 1


def _split(x, pieces):
    parts = []
    rem = x
    for i in range(pieces):
        part = rem.astype(BF16)
        parts.append(part)
        if i + 1 < pieces:
            rem = rem - part.astype(F32)
    return parts


def _dot(a_parts, b_parts, dims):
    order = max(len(a_parts), len(b_parts))
    acc = None
    for i, a in enumerate(a_parts):
        for j, b in enumerate(b_parts):
            if i + j < order:
                term = lax.dot_general(a, b, dims, preferred_element_type=F32)
                acc = term if acc is None else acc + term
    return acc


def _bd2(x):
    lane = lax.broadcasted_iota(jnp.int32, x.shape, 1)
    zero = jnp.zeros_like(x)
    return jnp.concatenate([jnp.where(lane < HEAD_DIM, x, zero),
                            jnp.where(lane >= HEAD_DIM, x, zero)], axis=0)


def _pair_nn(a, b, pieces):
    return _dot(_split(a, pieces), [_bd2(p) for p in _split(b, pieces)], _NN)


def _pair_nt(a, b, pieces):
    return _dot(_split(a, pieces), [_bd2(p) for p in _split(b, pieces)], _NT)


def _pair_tn(a, b, pieces):
    full = _dot(_split(a, pieces), _split(b, pieces), _TN)
    lane = lax.broadcasted_iota(jnp.int32, (HEAD_DIM, PAIR), 1)
    return jnp.where(lane < HEAD_DIM, full[0:HEAD_DIM], full[HEAD_DIM:])


def _seg_sum(x, seg):
    return _dot(_split(x, 2), [seg], _NN)


def _rwkv_kernel(y_ref, mu_ref, w0_ref, wdec_ref, a0_ref, waaa_ref, wgate_ref, kk_ref, ka_ref,
                 rk_ref, gng_ref, gnb_ref, seg_ref, o_ref, state, carry, *, tile):
    t = pl.program_id(1)

    @pl.when(t == 0)
    def _():
        state[...] = jnp.zeros_like(state)
        carry[...] = jnp.zeros_like(carry)

    seg = seg_ref[...]
    y = y_ref[0]
    rows_i = lax.broadcasted_iota(jnp.int32, y.shape, 0)
    prev = jnp.where(rows_i == 0, carry[...], pltpu.roll(y, 1, 0))
    carry[...] = y[tile - 1:tile, :]
    ys = y + mu_ref[...] * (prev - y)

    w = RWKV_WIDTH
    r = ys[:, 0:w]
    kr = ys[:, w:2 * w]
    vr = ys[:, 2 * w:3 * w]
    dw = ys[:, 3 * w:3 * w + DECAY_LORA]
    da = ys[:, 3 * w + DECAY_LORA:3 * w + DECAY_LORA + AAA_LORA]
    dg = ys[:, 3 * w + DECAY_LORA + AAA_LORA:]

    w_logit = w0_ref[...] + jnp.dot(jnp.tanh(dw).astype(BF16), wdec_ref[...],
                                    preferred_element_type=F32)
    ld = -DECAY_SCALE * jax.nn.sigmoid(w_logit)
    a = jax.nn.sigmoid(a0_ref[...] + jnp.dot(da.astype(BF16), waaa_ref[...],
                                             preferred_element_type=F32))
    g = jnp.dot(jax.nn.sigmoid(dg).astype(BF16), wgate_ref[...], preferred_element_type=F32)

    kk = kr * kk_ref[...]
    kk = kk / jnp.maximum(jnp.sqrt(_seg_sum(kk * kk, seg)), L2_EPS)
    k_mod = kr * (1.0 + (a - 1.0) * ka_ref[...])
    a_vec = -kk
    b_vec = kk * a

    n_chunks = tile // CHUNK
    ti = lax.broadcasted_iota(jnp.int32, (tile, tile), 0)
    si = lax.broadcasted_iota(jnp.int32, (tile, tile), 1)
    tri = ((ti // CHUNK == si // CHUNK) & (si <= ti)).astype(BF16)
    cum = _dot([tri], _split(ld, 3), _NN)
    cum_end = jnp.concatenate(
        [jnp.broadcast_to(cum[(c + 1) * CHUNK - 1:(c + 1) * CHUNK, :], (CHUNK, w))
         for c in range(n_chunks)], axis=0)
    p_in = jnp.exp(cum)
    inv_p = jnp.exp(-cum)
    to_end = jnp.exp(cum_end - cum)
    at = a_vec * jnp.exp(cum - ld)
    rt = r * p_in
    bt = b_vec * inv_p
    kt = k_mod * inv_p
    b_end = b_vec * to_end
    k_end = k_mod * to_end

    row = lax.broadcasted_iota(jnp.int32, (CHUNK, PAIR), 0)
    col = lax.broadcasted_iota(jnp.int32, (CHUNK, PAIR), 1) % HEAD_DIM
    strict = col < row
    incl = col <= row
    eye = (col == row).astype(F32)
    level_masks = [((row // (2 * s)) == (col // (2 * s))) & ((row % (2 * s)) >= s) & ((col % (2 * s)) < s)
                   for s in (1, 2, 4, 8, 16, 32)]

    probs = [(c, hp) for c in range(n_chunks) for hp in range(N_PAIRS)]

    def blk(x, prob):
        c, hp = prob
        return x[c * CHUNK:(c + 1) * CHUNK, hp * PAIR:(hp + 1) * PAIR]

    def masked(mask, x):
        return jnp.where(mask, x, 0.0)

    def bd2b(x):
        return _bd2(x.astype(BF16))

    def dot1(a, b, dims=_NN):
        return lax.dot_general(a.astype(BF16), b, dims, preferred_element_type=F32)

    lhs = [jnp.concatenate([blk(at, pr), blk(rt, pr)], axis=0) for pr in probs]
    sc = [dot1(l, jnp.concatenate([bd2b(blk(bt, pr)), bd2b(blk(kt, pr))], axis=0), _NT)
          for l, pr in zip(lhs, probs)]
    n_ab = [masked(strict, m[0:CHUNK, 0:PAIR]) for m in sc]
    a_ak = [masked(strict, m[0:CHUNK, PAIR:]) for m in sc]
    a_rb = [masked(incl, m[CHUNK:, 0:PAIR]) for m in sc]
    a_rk = [masked(incl, m[CHUNK:, PAIR:]) for m in sc]

    tinv = [eye + masked(level_masks[0], n) for n in n_ab]
    for lvl in range(1, len(level_masks)):
        x = [_pair_nn(masked(level_masks[lvl], n), ti_, PIECES_INVERSE) for n, ti_ in zip(n_ab, tinv)]
        tinv = [ti_ + _pair_nn(ti_, x_, PIECES_INVERSE) for ti_, x_ in zip(tinv, x)]

    akv = [dot1(jnp.concatenate([ak, rk], axis=0), bd2b(blk(vr, pr)))
           for ak, rk, pr in zip(a_ak, a_rk, probs)]
    tax = [dot1(ti_, jnp.concatenate([bd2b(blk(at, pr)), bd2b(m[0:CHUNK])], axis=1))
           for ti_, m, pr in zip(tinv, akv, probs)]
    a_bar = [m[:, 0:PAIR] for m in tax]
    x1 = [m[:, PAIR:] for m in tax]
    rbx = [dot1(m, jnp.concatenate([bd2b(ab), bd2b(x_)], axis=1)) for m, ab, x_ in zip(a_rb, a_bar, x1)]
    r_bar = [blk(rt, pr) + m[:, 0:PAIR] for m, pr in zip(rbx, probs)]
    o1 = [m[:, PAIR:] + kv[CHUNK:] for m, kv in zip(rbx, akv)]
    g_mat = [eye * blk(p_in, pr)[CHUNK - 1:CHUNK, :] + _pair_tn(ab, blk(b_end, pr), PIECES_CHUNK_OPS)
             for ab, pr in zip(a_bar, probs)]
    z_mat = [_pair_tn(jnp.concatenate([x_, blk(vr, pr)], axis=0),
                      jnp.concatenate([blk(b_end, pr), blk(k_end, pr)], axis=0), PIECES_CHUNK_OPS)
             for x_, pr in zip(x1, probs)]

    s_cur = [state[hp] for hp in range(N_PAIRS)]
    o_rows = []
    for c in range(n_chunks):
        o_c = []
        for hp in range(N_PAIRS):
            i = c * N_PAIRS + hp
            o_c.append(_pair_nt(r_bar[i], s_cur[hp], PIECES_SCAN) + o1[i])
        s_cur = [_pair_nn(s_cur[hp], g_mat[c * N_PAIRS + hp], PIECES_SCAN) + z_mat[c * N_PAIRS + hp]
                 for hp in range(N_PAIRS)]
        o_rows.append(jnp.concatenate(o_c, axis=1))
    for hp in range(N_PAIRS):
        state[hp] = s_cur[hp]
    o = jnp.concatenate(o_rows, axis=0)

    inv_n = 1.0 / HEAD_DIM
    mean = _seg_sum(o, seg) * inv_n
    d = o - mean
    var = _seg_sum(d * d, seg) * inv_n
    on = d * lax.rsqrt(var + GN_EPS) * gng_ref[...] + gnb_ref[...]
    bonus = _seg_sum(r * k_mod * rk_ref[...], seg) * vr
    o_ref[0] = ((on + bonus) * g).astype(BF16)


def _rwkv(y_rw, p, tile):
    b, s, cols = y_rw.shape
    w = RWKV_WIDTH
    row = lambda t: t.reshape(1, -1).astype(F32)
    lanes = jnp.arange(w) // HEAD_DIM
    seg = (lanes[:, None] == lanes[None, :]).astype(BF16)
    vec = lambda n: _const_spec((1, n))
    return pl.pallas_call(
        functools.partial(_rwkv_kernel, tile=tile),
        grid=(b, s // tile),
        in_specs=[
            pl.BlockSpec((1, tile, cols), lambda bi, t: (bi, t, 0)),
            vec(cols), vec(w), _const_spec((DECAY_LORA, w)), vec(w), _const_spec((AAA_LORA, w)),
            _const_spec((GATE_LORA, w)), vec(w), vec(w), vec(w), vec(w), vec(w),
            _const_spec((w, w)),
        ],
        out_specs=pl.BlockSpec((1, tile, w), lambda bi, t: (bi, t, 0)),
        out_shape=jax.ShapeDtypeStruct((b, s, w), BF16),
        scratch_shapes=[
            pltpu.VMEM((N_PAIRS, HEAD_DIM, PAIR), F32),
            pltpu.VMEM((1, cols), F32),
        ],
        compiler_params=pltpu.CompilerParams(
            dimension_semantics=("arbitrary", "arbitrary"), vmem_limit_bytes=VMEM_LIMIT),
        name="rwkv7",
    )(y_rw, row(p["mu_shift"]), row(p["w0"]), p["w_decay_up"].astype(BF16), row(p["a0"]),
      p["w_aaa_up"].astype(BF16), p["w_gate_up"].astype(BF16), row(p["k_k"]), row(p["k_a"]),
      row(p["r_k"]), row(p["gn_g"]), row(p["gn_b"]), seg)


def _layer_norm(x, g, b):
    mu = jnp.mean(x, axis=-1, keepdims=True)
    d = x - mu
    var = jnp.mean(d * d, axis=-1, keepdims=True)
    return d * lax.rsqrt(var + LN_EPS) * g + b


def _tail_kernel(oatt_ref, orw_ref, x_ref, wout_ref, g1_ref, b1_ref,
                 wg_ref, wu_ref, wd_ref, g2_ref, b2_ref, out_ref, *, alpha):
    mix = (jnp.dot(oatt_ref[...], wout_ref[0:ATT_WIDTH, :], preferred_element_type=F32)
           + jnp.dot(orw_ref[...], wout_ref[ATT_WIDTH:, :], preferred_element_type=F32))
    h = _layer_norm(alpha * x_ref[...] + mix, g1_ref[...], b1_ref[...])
    hb = h.astype(BF16)
    gate = jnp.dot(hb, wg_ref[...], preferred_element_type=F32)
    up = jnp.dot(hb, wu_ref[...], preferred_element_type=F32)
    act = (gate * jax.nn.sigmoid(gate) * up).astype(BF16)
    ffn = jnp.dot(act, wd_ref[...], preferred_element_type=F32)
    out_ref[...] = _layer_norm(alpha * h + ffn, g2_ref[...], b2_ref[...])


def _tail(o_att, o_rw, x2d, p, alpha, tile):
    n, d_model = x2d.shape
    hidden = p["w_ffn_gate"].shape[1]
    row = lambda i: (i, 0)
    vec = lambda t: t.reshape(1, -1).astype(F32)
    return pl.pallas_call(
        functools.partial(_tail_kernel, alpha=alpha),
        grid=(n // tile,),
        in_specs=[
            pl.BlockSpec((tile, ATT_WIDTH), row),
            pl.BlockSpec((tile, RWKV_WIDTH), row),
            pl.BlockSpec((tile, d_model), row),
            _const_spec((ATT_WIDTH + RWKV_WIDTH, d_model)),
            _const_spec((1, d_model)), _const_spec((1, d_model)),
            _const_spec((d_model, hidden)), _const_spec((d_model, hidden)),
            _const_spec((hidden, d_model)),
            _const_spec((1, d_model)), _const_spec((1, d_model)),
        ],
        out_specs=pl.BlockSpec((tile, d_model), row),
        out_shape=jax.ShapeDtypeStruct((n, d_model), F32),
        compiler_params=pltpu.CompilerParams(
            dimension_semantics=("arbitrary",), vmem_limit_bytes=VMEM_LIMIT),
        name="merge_outproj_ffn",
    )(o_att, o_rw, x2d, p["w_out"].astype(BF16), vec(p["ln_mix_g"]), vec(p["ln_mix_b"]),
      p["w_ffn_gate"].astype(BF16), p["w_ffn_up"].astype(BF16), p["w_ffn_down"].astype(BF16),
      vec(p["ln_ffn_g"]), vec(p["ln_ffn_b"]))


def kernel(x, positions, w_in, w_out, mu_shift, w0, w_decay_up, a0, w_aaa_up, w_gate_up, k_k, k_a,
           r_k, gn_g, gn_b, ln_mix_g, ln_mix_b, w_ffn_gate, w_ffn_up, w_ffn_down, ln_ffn_g,
           ln_ffn_b):
    b, s, d_model = x.shape
    depth = w_in.shape[0]
    alpha = (2.0 * depth) ** 0.25
    assert s % 2048 == 0 and d_model == 2 * ATT_WIDTH
    params = dict(w_in=w_in, w_out=w_out, mu_shift=mu_shift, w0=w0, w_decay_up=w_decay_up, a0=a0,
                  w_aaa_up=w_aaa_up, w_gate_up=w_gate_up, k_k=k_k, k_a=k_a, r_k=r_k, gn_g=gn_g,
                  gn_b=gn_b, ln_mix_g=ln_mix_g, ln_mix_b=ln_mix_b, w_ffn_gate=w_ffn_gate,
                  w_ffn_up=w_ffn_up, w_ffn_down=w_ffn_down, ln_ffn_g=ln_ffn_g, ln_ffn_b=ln_ffn_b)
    h2d = x.reshape(b * s, d_model)
    for layer in range(depth):
        p = {name: t[layer] for name, t in params.items()}
        qkv, y_rw = _inproj(h2d.reshape(b, s, d_model), p["w_in"].astype(BF16), positions, tile=512)
        o_att = _dilated_attention(qkv).reshape(b * s, ATT_WIDTH)
        o_rw = _rwkv(y_rw, p, tile=256).reshape(b * s, RWKV_WIDTH)
        h2d = _tail(o_att, o_rw, h2d, p, alpha, tile=512)
    return h2d.reshape(b, s, d_model)
```
